```python
import math
import jax, jax.numpy as jnp
from jax import lax
import numpy as np

D_MODEL = 1024
BATCH = 4
SEQ = 4096
DEPTH = 1

DN_HEADS = 8
DN_HEAD_DIM = 128
DN_WIDTH = DN_HEADS * DN_HEAD_DIM
CONV_K = 5
CHUNK = 64
MLA_HEADS = 8
Q_LORA = 512
KV_LORA = 256
NOPE_DIM = 128
ROPE_DIM = 64
V_DIM = 128
QK_DIM = NOPE_DIM + ROPE_DIM
MLA_WIDTH = MLA_HEADS * V_DIM
ROPE_THETA = 10000.0
Q_BLOCK = 128
N_EXPERTS = 16
CAPACITY_FACTOR = 2
D_EXPERT = 1024
N_BRANCHES = 2
N_MOD = 6
EPS = 1e-6

IN_SPLITS = (3 * DN_WIDTH,
             DN_WIDTH,
             2 * DN_HEADS,
             2 * DN_HEADS,
             Q_LORA,
             KV_LORA,
             ROPE_DIM,
             N_BRANCHES * D_MODEL)
N_IN = sum(IN_SPLITS)

kernel_name = 'hybrid_gdn_mla_ec_block'


def rms_norm(x, g):
    xf = x.astype(jnp.float32)
    y = xf * lax.rsqrt(jnp.mean(xf * xf, axis=-1, keepdims=True) + EPS)
    return (y * g.astype(jnp.float32)).astype(x.dtype)


def l2_norm(x):
    return x * lax.rsqrt(jnp.sum(x * x, axis=-1, keepdims=True) + EPS)


def apply_rope(x, cos, sin):
    x1, x2 = jnp.split(x.astype(jnp.float32), 2, axis=-1)
    return jnp.concatenate([x1 * cos - x2 * sin, x2 * cos + x1 * sin], axis=-1).astype(x.dtype)


def centred_depthwise_conv(x, w):
    ch = x.shape[-1]
    pad = (CONV_K - 1) // 2
    return lax.conv_general_dilated(x, w[:, None, :].astype(x.dtype), window_strides=(1,),
                                    padding=[(pad, pad)], dimension_numbers=('NWC', 'WIO', 'NWC'),
                                    feature_group_count=ch)


def chunk_gated_delta_rule(q, k, v, g, beta):
    b, h, s, dk = k.shape
    dv = v.shape[-1]
    nc = s // CHUNK
    q = q * (dk ** -0.5)
    qc = q.reshape(b, h, nc, CHUNK, dk)
    kc = k.reshape(b, h, nc, CHUNK, dk)
    vc = v.reshape(b, h, nc, CHUNK, dv)
    bc = beta.reshape(b, h, nc, CHUNK)
    gc = jnp.cumsum(g.reshape(b, h, nc, CHUNK), axis=-1)
    tril = jnp.tril(jnp.ones((CHUNK, CHUNK), dtype=bool))
    strict = jnp.tril(jnp.ones((CHUNK, CHUNK), dtype=bool), -1)
    diff = gc[..., :, None] - gc[..., None, :]
    decay = jnp.where(tril, jnp.exp(jnp.where(tril, diff, 0.0)), 0.0)
    k_beta = kc * bc[..., None]
    v_beta = vc * bc[..., None]
    lower = jnp.where(strict, jnp.einsum('bhnik,bhnjk->bhnij', k_beta, kc) * decay, 0.0)
    eye = jnp.eye(CHUNK, dtype=jnp.float32)
    t_inv = lax.linalg.triangular_solve(eye + lower, jnp.broadcast_to(eye, lower.shape),
                                        left_side=True, lower=True, unit_diagonal=True)
    u = t_inv @ v_beta
    w = t_inv @ (k_beta * jnp.exp(gc)[..., None])
    intra = jnp.where(tril, jnp.einsum('bhnik,bhnjk->bhnij', qc, kc) * decay, 0.0)
    q_dec = qc * jnp.exp(gc)[..., None]
    g_last = gc[..., -1]
    k_state = kc * jnp.exp(g_last[..., None] - gc)[..., None]

    def step(state, xs):
        w_i, u_i, intra_i, qd_i, ks_i, gl_i = xs
        v_new = u_i - w_i @ state
        o_i = qd_i @ state + intra_i @ v_new
        state = state * jnp.exp(gl_i)[..., None, None] + jnp.swapaxes(ks_i, -1, -2) @ v_new
        return state, o_i

    xs = tuple(jnp.moveaxis(t, 2, 0) for t in (w, u, intra, q_dec, k_state, g_last))
    state0 = jnp.zeros((b, h, dk, dv), jnp.float32)
    _, o = lax.scan(step, state0, xs)
    return jnp.moveaxis(o, 0, 2).reshape(b, h, s, dv)


def gated_deltanet_branch(qkv, z, b_in, a_in, conv_w, a_log, dt_bias, o_gain):
    bsz, s, _ = qkv.shape
    qkv = jax.nn.silu(centred_depthwise_conv(qkv, conv_w))
    q, k, v = jnp.split(qkv, 3, axis=-1)

    def heads(t):
        return t.reshape(bsz, s, DN_HEADS, DN_HEAD_DIM).transpose(0, 2, 1, 3).astype(jnp.float32)

    q, k, v = l2_norm(heads(q)), l2_norm(heads(k)), heads(v)
    beta = jax.nn.sigmoid(b_in.astype(jnp.float32)).reshape(bsz, s, 2, DN_HEADS).transpose(2, 0, 3, 1)
    a = a_in.astype(jnp.float32).reshape(bsz, s, 2, DN_HEADS)
    g = -jnp.exp(a_log.astype(jnp.float32)) * jax.nn.softplus(a + dt_bias.astype(jnp.float32))
    g = g.transpose(2, 0, 3, 1)
    o_fwd = chunk_gated_delta_rule(q, k, v, g[0], beta[0])
    flip = lambda t: jnp.flip(t, axis=2)
    o_bwd = flip(chunk_gated_delta_rule(flip(q), flip(k), flip(v), flip(g[1]), flip(beta[1])))
    o = (o_fwd + o_bwd).transpose(0, 2, 1, 3)
    zf = z.astype(jnp.float32).reshape(bsz, s, DN_HEADS, DN_HEAD_DIM)
    o = rms_norm(o, o_gain) * jax.nn.silu(zf)
    return o.reshape(bsz, s, DN_WIDTH).astype(qkv.dtype)


def mla_branch(c_q, c_kv, k_r, q_gain, w_uq, kv_gain, w_ukv, cos, sin):
    bsz, s, _ = c_q.shape
    q = (rms_norm(c_q, q_gain) @ w_uq).reshape(bsz, s, MLA_HEADS, QK_DIM)
    q = jnp.concatenate([q[..., :NOPE_DIM],
                         apply_rope(q[..., NOPE_DIM:], cos[:, :, None], sin[:, :, None])], axis=-1)
    kv = (rms_norm(c_kv, kv_gain) @ w_ukv).reshape(bsz, s, MLA_HEADS, NOPE_DIM + V_DIM)
    k_nope, v = kv[..., :NOPE_DIM], kv[..., NOPE_DIM:]
    k_rope = apply_rope(k_r, cos, sin)
    k = jnp.concatenate([k_nope, jnp.broadcast_to(k_rope[:, :, None], (bsz, s, MLA_HEADS, ROPE_DIM))], axis=-1)
    qb = (q * (QK_DIM ** -0.5)).reshape(bsz, s // Q_BLOCK, Q_BLOCK, MLA_HEADS, QK_DIM).transpose(1, 0, 2, 3, 4)

    def attend(q_blk):
        scores = jnp.einsum('bqhd,bkhd->bhqk', q_blk, k, preferred_element_type=jnp.float32)
        p = jax.nn.softmax(scores, axis=-1)
        return jnp.einsum('bhqk,bkhd->bqhd', p.astype(v.dtype), v)

    o = lax.map(attend, qb)
    return o.transpose(1, 0, 2, 3, 4).reshape(bsz, s, MLA_WIDTH)


def expert_choice_ffn(h, w_router, w_gate, w_up, w_down):
    bsz, s, d = h.shape
    cap = CAPACITY_FACTOR * s // N_EXPERTS
    aff = jax.nn.softmax((h @ w_router).astype(jnp.float32), axis=-1)
    gate, idx = lax.top_k(aff.transpose(0, 2, 1), cap)
    idx_flat = idx.reshape(bsz, N_EXPERTS * cap)
    xe = jnp.take_along_axis(h, idx_flat[..., None], axis=1).reshape(bsz, N_EXPERTS, cap, d)
    hid = jax.nn.silu(jnp.einsum('becd,edf->becf', xe, w_gate)) * jnp.einsum('becd,edf->becf', xe, w_up)
    ye = jnp.einsum('becf,efd->becd', hid, w_down) * gate[..., None].astype(h.dtype)
    out = jnp.zeros_like(h).at[jnp.arange(bsz)[:, None], idx_flat].add(ye.reshape(bsz, N_EXPERTS * cap, d))
    return out


def setup_inputs(seed: int = 0) -> dict:
    key = jax.random.key(seed)
    ks = jax.random.split(key, 24)
    f32 = jnp.float32

    def nrm(k, shape, fan_in):
        return jax.random.normal(k, shape, f32) * (fan_in ** -0.5)

    def gain(k, shape):
        return 1.0 + 0.05 * jax.random.normal(k, shape, f32)

    x = jax.random.normal(ks[0], (BATCH, SEQ, D_MODEL), f32)
    c = jax.random.normal(ks[1], (BATCH, D_MODEL), f32)
    positions = (jnp.cumsum(jax.random.randint(ks[2], (BATCH, SEQ), 1, 3), axis=1) - 1).astype(jnp.int32)
    w_mod = 0.5 * nrm(ks[3], (DEPTH, D_MODEL, N_MOD * D_MODEL), D_MODEL)
    b_mod = 0.02 * jax.random.normal(ks[4], (DEPTH, N_MOD * D_MODEL), f32)
    g_mix = gain(ks[5], (DEPTH, D_MODEL))
    w_in = nrm(ks[6], (DEPTH, D_MODEL, N_IN), D_MODEL)
    conv_w = nrm(ks[7], (DEPTH, CONV_K, 3 * DN_WIDTH), CONV_K)
    a_log = jnp.log(jax.random.uniform(ks[8], (DEPTH, 2, DN_HEADS), f32, 1.0, 16.0))
    dt = jnp.exp(jax.random.uniform(ks[9], (DEPTH, 2, DN_HEADS), f32, math.log(1e-3), math.log(1e-1)))
    dt_bias = dt + jnp.log(-jnp.expm1(-dt))
    dn_o_gain = gain(ks[10], (DEPTH, DN_HEAD_DIM))
    q_gain = gain(ks[11], (DEPTH, Q_LORA))
    w_uq = nrm(ks[12], (DEPTH, Q_LORA, MLA_HEADS * QK_DIM), Q_LORA)
    kv_gain = gain(ks[13], (DEPTH, KV_LORA))
    w_ukv = nrm(ks[14], (DEPTH, KV_LORA, MLA_HEADS * (NOPE_DIM + V_DIM)), KV_LORA)
    w_o_dn = nrm(ks[15], (DEPTH, DN_WIDTH, D_MODEL), DN_WIDTH)
    w_o_mla = nrm(ks[16], (DEPTH, MLA_WIDTH, D_MODEL), MLA_WIDTH)
    w_out = nrm(ks[17], (DEPTH, D_MODEL, D_MODEL), D_MODEL)
    g_ffn = gain(ks[18], (DEPTH, D_MODEL))
    w_router = nrm(ks[19], (DEPTH, D_MODEL, N_EXPERTS), D_MODEL)
    w_gate = nrm(ks[20], (DEPTH, N_EXPERTS, D_MODEL, D_EXPERT), D_MODEL)
    w_up = nrm(ks[21], (DEPTH, N_EXPERTS, D_MODEL, D_EXPERT), D_MODEL)
    w_down = nrm(ks[22], (DEPTH, N_EXPERTS, D_EXPERT, D_MODEL), D_EXPERT)
    g_final = gain(ks[23], (D_MODEL,))
    return {'x': x, 'c': c, 'positions': positions, 'w_mod': w_mod, 'b_mod': b_mod, 'g_mix': g_mix,
            'w_in': w_in, 'conv_w': conv_w, 'a_log': a_log, 'dt_bias': dt_bias, 'dn_o_gain': dn_o_gain,
            'q_gain': q_gain, 'w_uq': w_uq, 'kv_gain': kv_gain, 'w_ukv': w_ukv, 'w_o_dn': w_o_dn,
            'w_o_mla': w_o_mla, 'w_out': w_out, 'g_ffn': g_ffn, 'w_router': w_router, 'w_gate': w_gate,
            'w_up': w_up, 'w_down': w_down, 'g_final': g_final}


def reference(x, c, positions, w_mod, b_mod, g_mix, w_in, conv_w, a_log, dt_bias, dn_o_gain,
              q_gain, w_uq, kv_gain, w_ukv, w_o_dn, w_o_mla, w_out, g_ffn, w_router, w_gate,
              w_up, w_down, g_final):
    half = ROPE_DIM // 2
    inv_freq = ROPE_THETA ** (-jnp.arange(half, dtype=jnp.float32) / half)
    ang = positions.astype(jnp.float32)[..., None] * inv_freq
    cos, sin = jnp.cos(ang), jnp.sin(ang)
    split_at = np.cumsum(IN_SPLITS)[:-1].tolist()
    for l in range(DEPTH):
        mod = jax.nn.silu(c) @ w_mod[l] + b_mod[l]
        sh1, sc1, gt1, sh2, sc2, gt2 = jnp.split(mod[:, None, :], N_MOD, axis=-1)
        h = rms_norm(x, g_mix[l]) * (1.0 + sc1) + sh1
        proj = h @ w_in[l]
        dn_qkv, dn_z, dn_b, dn_a, c_q, c_kv, k_r, gates = jnp.split(proj, split_at, axis=-1)
        y_dn = gated_deltanet_branch(dn_qkv, dn_z, dn_b, dn_a, conv_w[l], a_log[l], dt_bias[l],
                                     dn_o_gain[l]) @ w_o_dn[l]
        y_mla = mla_branch(c_q, c_kv, k_r, q_gain[l], w_uq[l], kv_gain[l], w_ukv[l], cos, sin) @ w_o_mla[l]
        g_dn, g_mla = jnp.split(jax.nn.sigmoid(gates), N_BRANCHES, axis=-1)
        merged = g_dn * y_dn + g_mla * y_mla
        x = x + gt1 * (merged @ w_out[l])
        h = rms_norm(x, g_ffn[l]) * (1.0 + sc2) + sh2
        x = x + gt2 * expert_choice_ffn(h, w_router[l], w_gate[l], w_up[l], w_down[l])
    return rms_norm(x, g_final)
```

```python
import functools
import math

import jax
import jax.numpy as jnp
import numpy as np
from jax import lax
from jax.experimental import pallas as pl
from jax.experimental.pallas import tpu as pltpu

F32 = jnp.float32
BF16 = jnp.bfloat16
I32 = jnp.int32

EPS = 1e-6
DN_HEADS = 8
DN_HEAD_DIM = 128
CONV_K = 5
CHUNK = 64
PAIR = 2 * CHUNK
MLA_HEADS = 8
NOPE_DIM = 128
ROPE_DIM = 64
V_DIM = 128
QK_DIM = NOPE_DIM + ROPE_DIM
Q_LORA = 512
KV_LORA = 256
ROPE_THETA = 10000.0
N_EXPERTS = 16
CAPACITY_FACTOR = 2
LANES = 128
VMEM_LIMIT = 56 * 1024 * 1024

COL_Q, COL_K, COL_V, COL_Z = 0, 1024, 2048, 3072
COL_CQ, COL_CKV, COL_KR, COL_BA, COL_GATE = 4096, 4608, 4864, 4992, 5120
N_PAD = 7168


def _cparams(sem):
    return pltpu.CompilerParams(dimension_semantics=sem, vmem_limit_bytes=VMEM_LIMIT)


def _dot(a, b):
    return jnp.dot(a, b, preferred_element_type=F32)


def _dot_nt(a, b):
    return lax.dot_general(a, b, (((1,), (1,)), ((), ())), preferred_element_type=F32)


def _dot_tn(a, b):
    return lax.dot_general(a, b, (((0,), (0,)), ((), ())), preferred_element_type=F32)


def _sigmoid(x):
    return 1.0 / (1.0 + jnp.exp(-x))


def _softplus(x):
    return jnp.maximum(x, 0.0) + jnp.log(1.0 + jnp.exp(-jnp.abs(x)))


def _split2(x):
    h = x.astype(BF16)
    return h, (x - h.astype(F32)).astype(BF16)


def _split3(x):
    h = x.astype(BF16)
    r = x - h.astype(F32)
    m = r.astype(BF16)
    return h, m, (r - m.astype(F32)).astype(BF16)


def _dot3_parts(a_parts, b_parts):
    ah, al = a_parts
    bh, bl = b_parts
    return _dot(jnp.concatenate([ah, ah, al], axis=1), jnp.concatenate([bh, bl, bh], axis=0))


def _mod_kernel(c_ref, w_ref, b_ref, o_ref):
    c = c_ref[...]
    s = c * _sigmoid(c)
    o_ref[...] = _dot(s.astype(BF16), w_ref[...].astype(BF16)) + b_ref[...]


def _mod_call(c_pad, w_mod, b_mod):
    rows, d = c_pad.shape
    n = w_mod.shape[1]
    tn = 1024
    return pl.pallas_call(
        _mod_kernel,
        grid=(n // tn,),
        in_specs=[pl.BlockSpec((rows, d), lambda j: (0, 0)),
                  pl.BlockSpec((d, tn), lambda j: (0, j)),
                  pl.BlockSpec((1, tn), lambda j: (0, j))],
        out_specs=pl.BlockSpec((rows, tn), lambda j: (0, j)),
        out_shape=jax.ShapeDtypeStruct((rows, n), F32),
        compiler_params=_cparams(("arbitrary",)),
    )(c_pad, w_mod, b_mod.reshape(1, n))


def _inproj_kernel(x_ref, mod_ref, g_ref, w_ref, o_ref, h_scr):
    @pl.when(pl.program_id(2) == 0)
    def _():
        x = x_ref[0]
        ms = jnp.mean(x * x, axis=-1, keepdims=True)
        y = x * lax.rsqrt(ms + EPS) * g_ref[...]
        h = y * (1.0 + mod_ref[0, 1:2, :]) + mod_ref[0, 0:1, :]
        h_scr[...] = h.astype(BF16)

    o_ref[0] = _dot(h_scr[...], w_ref[...])


def _inproj_call(x, mod3, g_mix, w_in_p):
    b, s, d = x.shape
    n = w_in_p.shape[1]
    tm = min(s, 1024)
    tn = 1024
    return pl.pallas_call(
        _inproj_kernel,
        grid=(b, s // tm, n // tn),
        in_specs=[pl.BlockSpec((1, tm, d), lambda bi, i, j: (bi, i, 0)),
                  pl.BlockSpec((1, 6, d), lambda bi, i, j: (bi, 0, 0)),
                  pl.BlockSpec((1, d), lambda bi, i, j: (0, 0)),
                  pl.BlockSpec((d, tn), lambda bi, i, j: (0, j))],
        out_specs=pl.BlockSpec((1, tm, tn), lambda bi, i, j: (bi, i, j)),
        out_shape=jax.ShapeDtypeStruct((b, s, n), F32),
        scratch_shapes=[pltpu.VMEM((tm, d), BF16)],
        compiler_params=_cparams(("parallel", "parallel", "arbitrary")),
    )(x, mod3, g_mix.reshape(1, d), w_in_p)


def _dn_kernel(q_ref, k_ref, v_ref, z_ref, ba_ref, cwq_ref, cwk_ref, cwv_ref, alog_ref, dt_ref, og_ref,
               y_ref,
               xp, qn, kn, vn, w_s, qd_s, kst_s, u_s, in_s, egl_s, o_s, *, seq, tr):
    head = pl.program_id(1)
    n_pairs = seq // PAIR
    n_tiles = seq // tr

    def conv_phase(x_ref, cw_ref, dst, normalise, scale):
        xp[0:8, :] = jnp.zeros((8, LANES), F32)
        xp[seq + 8:seq + 16, :] = jnp.zeros((8, LANES), F32)

        def copy_tile(t, c):
            r0 = pl.multiple_of(t * tr, tr)
            xp[pl.ds(r0 + 8, tr), :] = x_ref[0, pl.ds(r0, tr), :]
            return c

        lax.fori_loop(0, n_tiles, copy_tile, 0)
        cw = cw_ref[...]

        def tile(t, c):
            r0 = pl.multiple_of(t * tr, tr)
            win = xp[pl.ds(r0, tr + 16), :]
            acc = win[6:6 + tr] * cw[0:1]
            for kk in range(1, CONV_K):
                acc = acc + win[6 + kk:6 + kk + tr] * cw[kk:kk + 1]
            y = acc * _sigmoid(acc)
            if normalise:
                y = y * lax.rsqrt(jnp.sum(y * y, axis=-1, keepdims=True) + EPS)
            if scale != 1.0:
                y = y * scale
            dst[pl.ds(r0, tr), :] = y
            return c

        lax.fori_loop(0, n_tiles, tile, 0)

    conv_phase(q_ref, cwq_ref, qn, True, DN_HEAD_DIM ** -0.5)
    conv_phase(k_ref, cwk_ref, kn, True, 1.0)
    conv_phase(v_ref, cwv_ref, vn, False, 1.0)

    ri = lax.broadcasted_iota(I32, (PAIR, PAIR), 0)
    ci = lax.broadcasted_iota(I32, (PAIR, PAIR), 1)
    same = (ri // CHUNK) == (ci // CHUNK)
    mask_incl = (same & (ci <= ri), same & (ci >= ri))
    mask_strict = (same & (ci < ri), same & (ci > ri))
    eye = jnp.where(ri == ci, 1.0, 0.0).astype(F32)
    lu = jnp.concatenate([jnp.where(mask_incl[0], 1.0, 0.0), jnp.where(mask_incl[1], 1.0, 0.0)],
                         axis=0).astype(BF16)
    lu3 = jnp.concatenate([lu, lu, lu], axis=1)
    r2 = lax.broadcasted_iota(I32, (LANES, 2 * LANES), 0)
    c2 = lax.broadcasted_iota(I32, (LANES, 2 * LANES), 1)

    def sel2(col_a, col_b):
        e = jnp.where(r2 == jnp.where(c2 < LANES, col_a, col_b), 1.0, 0.0).astype(BF16)
        return jnp.concatenate([e, e, e], axis=0)

    e_beta = sel2(head, DN_HEADS + head)
    e_gate = sel2(2 * DN_HEADS + head, 3 * DN_HEADS + head)
    alog_row = alog_ref[...]
    dt_row = dt_ref[...]

    def replicate2(x, e3):
        h, m, l = _split3(x)
        r = _dot(jnp.concatenate([h, m, l], axis=1), e3)
        return r[:, :LANES], r[:, LANES:]

    def neumann(lm):
        m = -lm
        p = eye + m
        mp = _split2(m)
        for _ in range(5):
            m = _dot3_parts(mp, mp)
            mp = _split2(m)
            p = p + _dot3_parts(_split2(p), mp)
        return p

    def pair_prep(p, c):
        r0 = pl.multiple_of(p * PAIR, PAIR)
        blk = ba_ref[0, pl.ds(r0, PAIR), :]
        sig = _sigmoid(blk)
        g = -jnp.exp(alog_row) * _softplus(blk + dt_row)
        gh, gm, gl = _split3(g)
        cs = _dot(lu3, jnp.concatenate([gh, gm, gl], axis=0))
        pre, suf = cs[:PAIR], cs[PAIR:]
        beta_f, beta_b = replicate2(sig, e_beta)
        gc_f, pre_b = replicate2(pre, e_gate)
        suf_f, gc_b = replicate2(suf, e_gate)
        g_f, g_b = replicate2(g, e_gate)
        per_dir = ((beta_f, gc_f, suf_f - g_f), (beta_b, gc_b, pre_b - g_b))

        k2 = kn[pl.ds(r0, PAIR), :]
        q2 = qn[pl.ds(r0, PAIR), :]
        v2 = vn[pl.ds(r0, PAIR), :]
        kb = k2.astype(BF16)
        kq = _dot_nt(jnp.concatenate([kb, q2.astype(BF16)], axis=0), kb)
        kk, qk = kq[:PAIR], kq[PAIR:]

        for d in range(2):
            beta, gc, ex = per_dir[d]
            m_in, m_st = mask_incl[d], mask_strict[d]
            dec = jnp.where(m_in, jnp.exp(jnp.where(m_in, gc - gc.T, 0.0)), 0.0)
            t_inv = neumann(jnp.where(m_st, beta * kk * dec, 0.0))
            eg = jnp.exp(gc)
            vb = (v2 * beta).astype(BF16)
            kbg = (k2 * beta * eg).astype(BF16)
            uw = _dot(t_inv.astype(BF16), jnp.concatenate([vb, kbg], axis=1))
            u_s[d, pl.ds(r0, PAIR), :] = uw[:, :LANES]
            w_s[d, pl.ds(r0, PAIR), :] = uw[:, LANES:].astype(BF16)
            qd_s[d, pl.ds(r0, PAIR), :] = (q2 * eg).astype(BF16)
            kst_s[d, pl.ds(r0, PAIR), :] = (k2 * jnp.exp(ex)).T.astype(BF16)
            intra = jnp.where(m_in, qk * dec, 0.0).astype(BF16)
            in_s[d, pl.ds(r0, CHUNK), :] = intra[:CHUNK, :CHUNK]
            in_s[d, pl.ds(r0 + CHUNK, CHUNK), :] = intra[CHUNK:, CHUNK:]
            etot = jnp.exp(gc + ex)
            e0 = pl.multiple_of(p * 16, 16)
            egl_s[d, pl.ds(e0, 8), :] = etot[0:8]
            egl_s[d, pl.ds(e0 + 8, 8), :] = etot[CHUNK:CHUNK + 8]
        return c

    lax.fori_loop(0, n_pairs, pair_prep, 0)

    def zero_tile(t, c):
        r0 = pl.multiple_of(t * tr, tr)
        o_s[pl.ds(r0, tr), :] = jnp.zeros((tr, LANES), F32)
        return c

    lax.fori_loop(0, n_tiles, zero_tile, 0)

    def chunk_step(d, pair, sub, state):
        pr0 = pl.multiple_of(pair * PAIR, PAIR)
        r0 = pl.multiple_of(pr0 + sub * CHUNK, CHUNK)
        lhs1 = jnp.concatenate([w_s[d, pl.ds(r0, CHUNK), :], qd_s[d, pl.ds(r0, CHUNK), :]], axis=0)
        res = _dot(lhs1, state.astype(BF16))
        v_new = u_s[d, pl.ds(r0, CHUNK), :] - res[:CHUNK]
        kst = kst_s[d, pl.ds(pr0, PAIR), sub * CHUNK:(sub + 1) * CHUNK]
        lhs2 = jnp.concatenate([in_s[d, pl.ds(r0, CHUNK), :], kst], axis=0)
        res2 = _dot(lhs2, v_new.astype(BF16))
        o_s[pl.ds(r0, CHUNK), :] = o_s[pl.ds(r0, CHUNK), :] + res[CHUNK:] + res2[:CHUNK]
        e0 = pl.multiple_of(pair * 16 + sub * 8, 8)
        eg = egl_s[d, pl.ds(e0, 8), :][0:1, :]
        return state * eg + res2[CHUNK:]

    def pair_scan(n, carry):
        s_f, s_b = carry
        pb = n_pairs - 1 - n
        s_f = chunk_step(0, n, 0, s_f)
        s_b = chunk_step(1, pb, 1, s_b)
        s_f = chunk_step(0, n, 1, s_f)
        s_b = chunk_step(1, pb, 0, s_b)
        return s_f, s_b

    zero_state = jnp.zeros((DN_HEAD_DIM, DN_HEAD_DIM), F32)
    lax.fori_loop(0, n_pairs, pair_scan, (zero_state, zero_state))

    og = og_ref[...]

    def out_tile(t, c):
        r0 = pl.multiple_of(t * tr, tr)
        o = o_s[pl.ds(r0, tr), :]
        z = z_ref[0, pl.ds(r0, tr), :]
        y = o * lax.rsqrt(jnp.mean(o * o, axis=-1, keepdims=True) + EPS) * og
        y_ref[0, pl.ds(r0, tr), :] = (y * (z * _sigmoid(z))).astype(BF16)
        return c

    lax.fori_loop(0, n_tiles, out_tile, 0)


def _dn_call(proj, conv_w, alog_row, dt_row, o_gain):
    b, s, _ = proj.shape
    hd = DN_HEAD_DIM
    tr = min(s, 256)
    qb, kb, vb, zb = COL_Q // hd, COL_K // hd, COL_V // hd, COL_Z // hd
    bab = COL_BA // LANES

    def col(base):
        return pl.BlockSpec((1, s, hd), lambda bi, h: (bi, 0, base + h))

    def cw(base):
        return pl.BlockSpec((CONV_K, hd), lambda bi, h: (0, base + h))

    row = pl.BlockSpec((1, LANES), lambda bi, h: (0, 0))
    kern = functools.partial(_dn_kernel, seq=s, tr=tr)
    return pl.pallas_call(
        kern,
        grid=(b, DN_HEADS),
        in_specs=[col(qb), col(kb), col(vb), col(zb),
                  pl.BlockSpec((1, s, LANES), lambda bi, h: (bi, 0, bab)),
                  cw(qb), cw(kb), cw(vb), row, row, row],
        out_specs=pl.BlockSpec((1, s, hd), lambda bi, h: (bi, 0, h)),
        out_shape=jax.ShapeDtypeStruct((b, s, DN_HEADS * hd), BF16),
        scratch_shapes=[pltpu.VMEM((s + 16, LANES), F32),
                        pltpu.VMEM((s, hd), F32), pltpu.VMEM((s, hd), F32), pltpu.VMEM((s, hd), F32),
                        pltpu.VMEM((2, s, hd), BF16),
                        pltpu.VMEM((2, s, hd), BF16),
                        pltpu.VMEM((2, s, PAIR), BF16),
                        pltpu.VMEM((2, s, hd), F32),
                        pltpu.VMEM((2, s, CHUNK), BF16),
                        pltpu.VMEM((2, s // CHUNK * 8, LANES), F32),
                        pltpu.VMEM((s, hd), F32)],
        compiler_params=_cparams(("parallel", "parallel")),
    )(proj, proj, proj, proj, proj, conv_w, conv_w, conv_w, alog_row, dt_row, o_gain.reshape(1, hd))


def _mlaproj_kernel(cq_ref, ckv_ref, kr_ref, pos_ref, invf_ref, sgn_ref, qg_ref, kvg_ref,
                    wqn_ref, wqr_ref, wqs_ref, wkn_ref, wv_ref,
                    q_ref, k_ref, v_ref):
    cq = cq_ref[0]
    hq = (cq * lax.rsqrt(jnp.mean(cq * cq, axis=-1, keepdims=True) + EPS) * qg_ref[...]).astype(BF16)
    qn = _dot(hq, wqn_ref[...])
    qr = _dot(hq, wqr_ref[...])
    qs = _dot(hq, wqs_ref[...])
    ang = pos_ref[0].astype(F32) * invf_ref[...]
    cc = jnp.cos(ang)
    ss = jnp.sin(ang) * sgn_ref[...]
    scale = QK_DIM ** -0.5
    ckv = ckv_ref[0]
    hkv = (ckv * lax.rsqrt(jnp.mean(ckv * ckv, axis=-1, keepdims=True) + EPS) * kvg_ref[...]).astype(BF16)
    kn = _dot(hkv, wkn_ref[...])
    vv = _dot(hkv, wv_ref[...])
    kr = kr_ref[0]
    k_rope = (kr[:, :ROPE_DIM] * cc + kr[:, ROPE_DIM:] * ss).astype(BF16)
    for h in range(MLA_HEADS):
        lo, hi = h * ROPE_DIM, (h + 1) * ROPE_DIM
        q_rope = qr[:, lo:hi] * cc + qs[:, lo:hi] * ss
        q_ref[0, h, :, 0:NOPE_DIM] = (qn[:, h * NOPE_DIM:(h + 1) * NOPE_DIM] * scale).astype(BF16)
        q_ref[0, h, :, NOPE_DIM:QK_DIM] = (q_rope * scale).astype(BF16)
        k_ref[0, h, :, 0:NOPE_DIM] = kn[:, h * NOPE_DIM:(h + 1) * NOPE_DIM].astype(BF16)
        k_ref[0, h, :, NOPE_DIM:QK_DIM] = k_rope
        v_ref[0, h] = vv[:, h * V_DIM:(h + 1) * V_DIM].astype(BF16)


def _mlaproj_call(proj, pos3, invf, sgn, q_gain, kv_gain, wqn, wqr, wqs, wkn, wv):
    b, s, _ = proj.shape
    tm = min(s, 512)
    nh = MLA_HEADS

    def full(a):
        return pl.BlockSpec(a.shape, lambda bi, i: (0,) * a.ndim)

    qg = q_gain.reshape(1, Q_LORA)
    kvg = kv_gain.reshape(1, KV_LORA)
    return pl.pallas_call(
        _mlaproj_kernel,
        grid=(b, s // tm),
        in_specs=[pl.BlockSpec((1, tm, Q_LORA), lambda bi, i: (bi, i, COL_CQ // Q_LORA)),
                  pl.BlockSpec((1, tm, KV_LORA), lambda bi, i: (bi, i, COL_CKV // KV_LORA)),
                  pl.BlockSpec((1, tm, LANES), lambda bi, i: (bi, i, COL_KR // LANES)),
                  pl.BlockSpec((1, tm, 1), lambda bi, i: (bi, i, 0)),
                  full(invf), full(sgn), full(qg), full(kvg),
                  full(wqn), full(wqr), full(wqs), full(wkn), full(wv)],
        out_specs=[pl.BlockSpec((1, nh, tm, QK_DIM), lambda bi, i: (bi, 0, i, 0)),
                   pl.BlockSpec((1, nh, tm, QK_DIM), lambda bi, i: (bi, 0, i, 0)),
                   pl.BlockSpec((1, nh, tm, V_DIM), lambda bi, i: (bi, 0, i, 0))],
        out_shape=[jax.ShapeDtypeStruct((b, nh, s, QK_DIM), BF16),
                   jax.ShapeDtypeStruct((b, nh, s, QK_DIM), BF16),
                   jax.ShapeDtypeStruct((b, nh, s, V_DIM), BF16)],
        compiler_params=_cparams(("parallel", "parallel")),
    )(proj, proj, proj, pos3, invf, sgn, qg, kvg, wqn, wqr, wqs, wkn, wv)


def _attn_kernel(q_ref, k_ref, v_ref, o_ref):
    s = _dot_nt(q_ref[0, 0], k_ref[0, 0])
    m = jnp.max(s, axis=-1, keepdims=True)
    p = jnp.exp(s - m)
    l = jnp.sum(p, axis=-1, keepdims=True)
    o = _dot(p.astype(BF16), v_ref[0, 0])
    o_ref[0] = (o / l).astype(BF16)


def _attn_call(q, k, v):
    b, nh, s, _ = q.shape
    tq = min(s, 256)
    return pl.pallas_call(
        _attn_kernel,
        grid=(b, nh, s // tq),
        in_specs=[pl.BlockSpec((1, 1, tq, QK_DIM), lambda bi, h, i: (bi, h, i, 0)),
                  pl.BlockSpec((1, 1, s, QK_DIM), lambda bi, h, i: (bi, h, 0, 0)),
                  pl.BlockSpec((1, 1, s, V_DIM), lambda bi, h, i: (bi, h, 0, 0))],
        out_specs=pl.BlockSpec((1, tq, V_DIM), lambda bi, h, i: (bi, i, h)),
        out_shape=jax.ShapeDtypeStruct((b, s, nh * V_DIM), BF16),
        compiler_params=_cparams(("parallel", "parallel", "arbitrary")),
    )(q, k, v)


def _merge_kernel(ydn_ref, ymla_ref, gdn_ref, gmla_ref, x_ref, mod_ref, wodn_ref, womla_ref, wout_ref,
                  gffn_ref, wrt_ref,
                  x1_ref, h2_ref, aff_ref):
    y_dn = _dot(ydn_ref[0], wodn_ref[...])
    y_mla = _dot(ymla_ref[0], womla_ref[...])
    merged = _sigmoid(gdn_ref[0]) * y_dn + _sigmoid(gmla_ref[0]) * y_mla
    x1 = x_ref[0] + mod_ref[0, 2:3, :] * _dot(merged.astype(BF16), wout_ref[...])
    x1_ref[0] = x1
    y = x1 * lax.rsqrt(jnp.mean(x1 * x1, axis=-1, keepdims=True) + EPS) * gffn_ref[...]
    h2 = (y * (1.0 + mod_ref[0, 4:5, :]) + mod_ref[0, 3:4, :]).astype(BF16)
    h2_ref[0] = h2
    logits = _dot_nt(wrt_ref[...], h2)
    mx = jnp.max(logits, axis=0, keepdims=True)
    ex = jnp.exp(logits - mx)
    aff_ref[0] = ex / jnp.sum(ex, axis=0, keepdims=True)


def _merge_call(y_dn, y_mla, proj, x, mod3, w_o_dn, w_o_mla, w_out, g_ffn, w_router_t):
    b, s, d = x.shape
    tm = min(s, 512)
    ne = w_router_t.shape[0]

    def tok(width):
        return pl.BlockSpec((1, tm, width), lambda bi, i: (bi, i, 0))

    def full(a):
        return pl.BlockSpec(a.shape, lambda bi, i: (0,) * a.ndim)

    gf = g_ffn.reshape(1, d)
    return pl.pallas_call(
        _merge_kernel,
        grid=(b, s // tm),
        in_specs=[tok(d), tok(d),
                  pl.BlockSpec((1, tm, d), lambda bi, i: (bi, i, COL_GATE // d)),
                  pl.BlockSpec((1, tm, d), lambda bi, i: (bi, i, COL_GATE // d + 1)),
                  tok(d),
                  pl.BlockSpec((1, 6, d), lambda bi, i: (bi, 0, 0)),
                  full(w_o_dn), full(w_o_mla), full(w_out), full(gf), full(w_router_t)],
        out_specs=[tok(d), tok(d), pl.BlockSpec((1, ne, tm), lambda bi, i: (bi, 0, i))],
        out_shape=[jax.ShapeDtypeStruct((b, s, d), F32),
                   jax.ShapeDtypeStruct((b, s, d), BF16),
                   jax.ShapeDtypeStruct((b, ne, s), F32)],
        compiler_params=_cparams(("parallel", "parallel")),
    )(y_dn, y_mla, proj, proj, x, mod3, w_o_dn, w_o_mla, w_out, gf, w_router_t)


def _route_kernel(aff_ref, pos_ref, *, seq, cap):
    a = aff_ref[0]
    ne = a.shape[0]
    bits = pltpu.bitcast(a, I32)
    capf = jnp.float32(cap)

    def count_ge(t):
        return jnp.sum(jnp.where(bits >= t, 1.0, 0.0), axis=1, keepdims=True)

    def search(_, carry):
        lo, hi = carry
        mid = lo + ((hi - lo + 1) >> 1)
        ok = count_ge(mid) >= capf
        return jnp.where(ok, mid, lo), jnp.where(ok, hi, mid - 1)

    lo0 = jnp.zeros((ne, 1), I32)
    hi0 = jnp.full((ne, 1), 0x7F800000, I32)
    thr, _ = lax.fori_loop(0, 32, search, (lo0, hi0))

    ur = lax.broadcasted_iota(I32, (LANES, LANES), 0)
    uc = lax.broadcasted_iota(I32, (LANES, LANES), 1)
    upper = jnp.where(ur <= uc, 1.0, 0.0).astype(BF16)

    def prefix_incl(mask):
        carry = jnp.zeros((ne, 1), F32)
        outs = []
        for j in range(seq // LANES):
            pr = _dot(mask[:, j * LANES:(j + 1) * LANES].astype(BF16), upper) + carry
            outs.append(pr)
            carry = pr[:, LANES - 1:LANES]
        return jnp.concatenate(outs, axis=1)

    gt = bits > thr
    eq = bits == thr
    n_gt = jnp.sum(jnp.where(gt, 1.0, 0.0), axis=1, keepdims=True)
    take_eq = eq & (prefix_incl(jnp.where(eq, 1.0, 0.0)) <= capf - n_gt)
    sel = gt | take_eq
    slot = prefix_incl(jnp.where(sel, 1.0, 0.0)) - 1.0
    pos_ref[0] = jnp.where(sel, slot, -1.0).astype(I32)


def _route_call(aff, cap):
    b, ne, s = aff.shape
    kern = functools.partial(_route_kernel, seq=s, cap=cap)
    return pl.pallas_call(
        kern,
        grid=(b,),
        in_specs=[pl.BlockSpec((1, ne, s), lambda bi: (bi, 0, 0))],
        out_specs=pl.BlockSpec((1, ne, s), lambda bi: (bi, 0, 0)),
        out_shape=jax.ShapeDtypeStruct((b, ne, s), I32),
        compiler_params=_cparams(("parallel",)),
    )(aff)


def _gather_kernel(pos_ref, aff_ref, h_ref, xe_ref, gate_ref, *, seq, cap, tk):
    slot = lax.broadcasted_iota(I32, (cap, 1), 0)
    acc = jnp.zeros((cap, h_ref.shape[-1]), F32)
    gacc = jnp.zeros((cap, 1), F32)
    for kt in range(seq // tk):
        pm = pos_ref[0, 0, :, kt * tk:(kt + 1) * tk]
        hit = pm == slot
        acc = acc + _dot(jnp.where(hit, 1.0, 0.0).astype(BF16), h_ref[0, kt * tk:(kt + 1) * tk, :])
        am = aff_ref[0, 0, :, kt * tk:(kt + 1) * tk]
        gacc = gacc + jnp.sum(jnp.where(hit, am, 0.0), axis=1, keepdims=True)
    xe_ref[0, 0] = acc.astype(BF16)
    gate_ref[0, 0] = jnp.broadcast_to(gacc, (cap, LANES))


def _gather_call(pos4, aff4, h2, cap):
    b, ne, _, s = pos4.shape
    d = h2.shape[-1]
    tk = min(s, 512)
    kern = functools.partial(_gather_kernel, seq=s, cap=cap, tk=tk)
    return pl.pallas_call(
        kern,
        grid=(b, ne),
        in_specs=[pl.BlockSpec((1, 1, 1, s), lambda bi, e: (bi, e, 0, 0)),
                  pl.BlockSpec((1, 1, 1, s), lambda bi, e: (bi, e, 0, 0)),
                  pl.BlockSpec((1, s, d), lambda bi, e: (bi, 0, 0))],
        out_specs=[pl.BlockSpec((1, 1, cap, d), lambda bi, e: (bi, e, 0, 0)),
                   pl.BlockSpec((1, 1, cap, LANES), lambda bi, e: (bi, e, 0, 0))],
        out_shape=[jax.ShapeDtypeStruct((b, ne, cap, d), BF16),
                   jax.ShapeDtypeStruct((b, ne, cap, LANES), F32)],
        compiler_params=_cparams(("parallel", "arbitrary")),
    )(pos4, aff4, h2)


def _ffn_kernel(xe_ref, gate_ref, wg_ref, wu_ref, wd_ref, ye_ref):
    x = xe_ref[0, 0]
    g = _dot(x, wg_ref[0])
    u = _dot(x, wu_ref[0])
    hid = (g * _sigmoid(g)) * u
    y = _dot(hid.astype(BF16), wd_ref[0])
    ye_ref[0, 0] = (y * gate_ref[0, 0][:, 0:1]).astype(BF16)


def _ffn_call(xe, gate, wg, wu, wd):
    b, ne, cap, d = xe.shape
    f = wg.shape[-1]
    return pl.pallas_call(
        _ffn_kernel,
        grid=(ne, b),
        in_specs=[pl.BlockSpec((1, 1, cap, d), lambda e, bi: (bi, e, 0, 0)),
                  pl.BlockSpec((1, 1, cap, LANES), lambda e, bi: (bi, e, 0, 0)),
                  pl.BlockSpec((1, d, f), lambda e, bi: (e, 0, 0)),
                  pl.BlockSpec((1, d, f), lambda e, bi: (e, 0, 0)),
                  pl.BlockSpec((1, f, d), lambda e, bi: (e, 0, 0))],
        out_specs=pl.BlockSpec((1, 1, cap, d), lambda e, bi: (bi, e, 0, 0)),
        out_shape=jax.ShapeDtypeStruct((b, ne, cap, d), BF16),
        compiler_params=_cparams(("parallel", "arbitrary")),
    )(xe, gate, wg, wu, wd)


def _combine_kernel(pos_ref, ye_ref, x1_ref, mod_ref, gfin_ref, o_ref, acc_ref, *, cap, last_norm):
    e = pl.program_id(2)

    @pl.when(e == 0)
    def _():
        acc_ref[...] = jnp.zeros_like(acc_ref)

    slot = lax.broadcasted_iota(I32, (cap, 1), 0)
    hit = pos_ref[0, 0] == slot
    acc_ref[...] += _dot_tn(jnp.where(hit, 1.0, 0.0).astype(BF16), ye_ref[0, 0])

    @pl.when(e == pl.num_programs(2) - 1)
    def _():
        x2 = x1_ref[0] + mod_ref[0, 5:6, :] * acc_ref[...]
        if last_norm:
            x2 = x2 * lax.rsqrt(jnp.mean(x2 * x2, axis=-1, keepdims=True) + EPS) * gfin_ref[...]
        o_ref[0] = x2


def _combine_call(pos4, ye, x1, mod3, g_final, cap, last_norm):
    b, s, d = x1.shape
    ne = ye.shape[1]
    tm = min(s, 512)
    kern = functools.partial(_combine_kernel, cap=cap, last_norm=last_norm)
    return pl.pallas_call(
        kern,
        grid=(b, s // tm, ne),
        in_specs=[pl.BlockSpec((1, 1, 1, tm), lambda bi, i, e: (bi, e, 0, i)),
                  pl.BlockSpec((1, 1, cap, d), lambda bi, i, e: (bi, e, 0, 0)),
                  pl.BlockSpec((1, tm, d), lambda bi, i, e: (bi, i, 0)),
                  pl.BlockSpec((1, 6, d), lambda bi, i, e: (bi, 0, 0)),
                  pl.BlockSpec((1, d), lambda bi, i, e: (0, 0))],
        out_specs=pl.BlockSpec((1, tm, d), lambda bi, i, e: (bi, i, 0)),
        out_shape=jax.ShapeDtypeStruct((b, s, d), F32),
        scratch_shapes=[pltpu.VMEM((tm, d), F32)],
        compiler_params=_cparams(("parallel", "parallel", "arbitrary")),
    )(pos4, ye, x1, mod3, g_final.reshape(1, d))


def _pack_w_in(w):
    o_ba = 4 * 1024 + 0
    o_b = o_ba
    o_a = o_b + 2 * DN_HEADS
    o_cq = o_a + 2 * DN_HEADS
    o_ckv = o_cq + Q_LORA
    o_kr = o_ckv + KV_LORA
    o_g = o_kr + ROPE_DIM
    half = ROPE_DIM // 2
    w_kr = w[:, o_kr:o_g]
    w_kr_sw = jnp.concatenate([w_kr[:, half:], w_kr[:, :half]], axis=1)
    pad = jnp.zeros((w.shape[0], COL_GATE - COL_BA - 4 * DN_HEADS), w.dtype)
    packed = jnp.concatenate([w[:, :o_ba], w[:, o_cq:o_ckv], w[:, o_ckv:o_kr], w_kr, w_kr_sw,
                              w[:, o_b:o_a], w[:, o_a:o_cq], pad, w[:, o_g:]], axis=1)
    assert packed.shape[1] == N_PAD
    return packed.astype(BF16)


def _lane_row(vals, offset):
    row = jnp.zeros((1, LANES), F32)
    return row.at[0, offset:offset + vals.size].set(vals.reshape(-1).astype(F32))


def kernel(x, c, positions, w_mod, b_mod, g_mix, w_in, conv_w, a_log, dt_bias, dn_o_gain, q_gain, w_uq,
           kv_gain, w_ukv, w_o_dn, w_o_mla, w_out, g_ffn, w_router, w_gate, w_up, w_down, g_final):
    b, s, d = x.shape
    depth = w_mod.shape[0]
    cap = CAPACITY_FACTOR * s // N_EXPERTS
    half = ROPE_DIM // 2
    inv_freq = ROPE_THETA ** (-jnp.arange(half, dtype=F32) / half)
    invf = jnp.concatenate([inv_freq, inv_freq]).reshape(1, ROPE_DIM)
    sgn = jnp.concatenate([-jnp.ones((half,), F32), jnp.ones((half,), F32)]).reshape(1, ROPE_DIM)
    pos3 = positions.reshape(b, s, 1)
    c_pad = jnp.zeros((8, d), F32).at[:b].set(c)
    swap = np.concatenate([np.arange(half, ROPE_DIM), np.arange(half)])

    for l in range(depth):
        mod3 = _mod_call(c_pad, w_mod[l], b_mod[l])[:b].reshape(b, 6, d)
        proj = _inproj_call(x, mod3, g_mix[l], _pack_w_in(w_in[l]))

        alog_row = _lane_row(a_log[l], 2 * DN_HEADS)
        dt_row = _lane_row(dt_bias[l], 2 * DN_HEADS)
        y_dn = _dn_call(proj, conv_w[l], alog_row, dt_row, dn_o_gain[l])

        wq = w_uq[l].reshape(Q_LORA, MLA_HEADS, QK_DIM)
        wqn = wq[:, :, :NOPE_DIM].reshape(Q_LORA, MLA_HEADS * NOPE_DIM).astype(BF16)
        wqr = wq[:, :, NOPE_DIM:].reshape(Q_LORA, MLA_HEADS * ROPE_DIM).astype(BF16)
        wqs = wq[:, :, NOPE_DIM:][:, :, swap].reshape(Q_LORA, MLA_HEADS * ROPE_DIM).astype(BF16)
        wkv = w_ukv[l].reshape(KV_LORA, MLA_HEADS, NOPE_DIM + V_DIM)
        wkn = wkv[:, :, :NOPE_DIM].reshape(KV_LORA, MLA_HEADS * NOPE_DIM).astype(BF16)
        wv = wkv[:, :, NOPE_DIM:].reshape(KV_LORA, MLA_HEADS * V_DIM).astype(BF16)
        q, k, v = _mlaproj_call(proj, pos3, invf, sgn, q_gain[l], kv_gain[l], wqn, wqr, wqs, wkn, wv)
        y_mla = _attn_call(q, k, v)

        x1, h2, aff = _merge_call(y_dn, y_mla, proj, x, mod3, w_o_dn[l].astype(BF16),
                                  w_o_mla[l].astype(BF16), w_out[l].astype(BF16), g_ffn[l],
                                  w_router[l].T.astype(BF16))
        pos = _route_call(aff, cap)
        pos4 = pos.reshape(b, N_EXPERTS, 1, s)
        aff4 = aff.reshape(b, N_EXPERTS, 1, s)
        xe, gate = _gather_call(pos4, aff4, h2, cap)
        ye = _ffn_call(xe, gate, w_gate[l].astype(BF16), w_up[l].astype(BF16), w_down[l].astype(BF16))
        x = _combine_call(pos4, ye, x1, mod3, g_final, cap, last_norm=(l == depth - 1))
    return x
```

```python
import functools
import math

import jax
import jax.numpy as jnp
import numpy as np
from jax import lax
from jax.experimental import pallas as pl
from jax.experimental.pallas import tpu as pltpu

F32 = jnp.float32
BF16 = jnp.bfloat16
I32 = jnp.int32

EPS = 1e-6
DN_HEADS = 8
DN_HEAD_DIM = 128
CONV_K = 5
CHUNK = 64
PAIR = 2 * CHUNK
AQ_ROWS = DN_HEAD_DIM + CHUNK
MLA_HEADS = 8
NOPE_DIM = 128
ROPE_DIM = 64
V_DIM = 128
QK_DIM = NOPE_DIM + ROPE_DIM
Q_LORA = 512
KV_LORA = 256
ROPE_THETA = 10000.0
N_EXPERTS = 16
CAPACITY_FACTOR = 2
COMBINE_GROUP = 4
LANES = 128
VMEM_LIMIT = 56 * 1024 * 1024

COL_Q, COL_K, COL_V, COL_Z = 0, 1024, 2048, 3072
COL_CQ, COL_CKV, COL_KR, COL_BA, COL_GATE = 4096, 4608, 4864, 4992, 5120
N_PAD = 7168


def _cparams(sem):
    return pltpu.CompilerParams(dimension_semantics=sem, vmem_limit_bytes=VMEM_LIMIT)


def _dot(a, b):
    return jnp.dot(a, b, preferred_element_type=F32)


def _dot_nt(a, b):
    return lax.dot_general(a, b, (((1,), (1,)), ((), ())), preferred_element_type=F32)


def _dot_tn(a, b):
    return lax.dot_general(a, b, (((0,), (0,)), ((), ())), preferred_element_type=F32)


def _sigmoid(x):
    return 1.0 / (1.0 + jnp.exp(-x))


def _softplus(x):
    return jnp.maximum(x, 0.0) + jnp.log(1.0 + jnp.exp(-jnp.abs(x)))


def _split2(x):
    h = x.astype(BF16)
    return h, (x - h.astype(F32)).astype(BF16)


def _split3(x):
    h = x.astype(BF16)
    r = x - h.astype(F32)
    m = r.astype(BF16)
    return h, m, (r - m.astype(F32)).astype(BF16)


def _dot3_parts(a_parts, b_parts):
    ah, al = a_parts
    bh, bl = b_parts
    return _dot(jnp.concatenate([ah, ah, al], axis=1), jnp.concatenate([bh, bl, bh], axis=0))


def _mod_kernel(c_ref, w_ref, b_ref, o_ref):
    c = c_ref[...]
    s = c * _sigmoid(c)
    o_ref[...] = _dot(s.astype(BF16), w_ref[...].astype(BF16)) + b_ref[...]


def _mod_call(c_pad, w_mod, b_mod):
    rows, d = c_pad.shape
    n = w_mod.shape[1]
    tn = 1024
    return pl.pallas_call(
        _mod_kernel,
        grid=(n // tn,),
        in_specs=[pl.BlockSpec((rows, d), lambda j: (0, 0)),
                  pl.BlockSpec((d, tn), lambda j: (0, j)),
                  pl.BlockSpec((1, tn), lambda j: (0, j))],
        out_specs=pl.BlockSpec((rows, tn), lambda j: (0, j)),
        out_shape=jax.ShapeDtypeStruct((rows, n), F32),
        compiler_params=_cparams(("arbitrary",)),
    )(c_pad, w_mod, b_mod.reshape(1, n))


def _inproj_kernel(x_ref, mod_ref, g_ref, w_ref, o_ref, h_scr):
    @pl.when(pl.program_id(2) == 0)
    def _():
        x = x_ref[0]
        ms = jnp.mean(x * x, axis=-1, keepdims=True)
        y = x * lax.rsqrt(ms + EPS) * g_ref[...]
        h = y * (1.0 + mod_ref[0, 1:2, :]) + mod_ref[0, 0:1, :]
        h_scr[...] = h.astype(BF16)

    o_ref[0] = _dot(h_scr[...], w_ref[...])


def _inproj_call(x, mod3, g_mix, w_in_p):
    b, s, d = x.shape
    n = w_in_p.shape[1]
    tm = min(s, 1024)
    tn = 1024
    return pl.pallas_call(
        _inproj_kernel,
        grid=(b, s // tm, n // tn),
        in_specs=[pl.BlockSpec((1, tm, d), lambda bi, i, j: (bi, i, 0)),
                  pl.BlockSpec((1, 6, d), lambda bi, i, j: (bi, 0, 0)),
                  pl.BlockSpec((1, d), lambda bi, i, j: (0, 0)),
                  pl.BlockSpec((d, tn), lambda bi, i, j: (0, j))],
        out_specs=pl.BlockSpec((1, tm, tn), lambda bi, i, j: (bi, i, j)),
        out_shape=jax.ShapeDtypeStruct((b, s, n), F32),
        scratch_shapes=[pltpu.VMEM((tm, d), BF16)],
        compiler_params=_cparams(("parallel", "parallel", "arbitrary")),
    )(x, mod3, g_mix.reshape(1, d), w_in_p)


def _dn_kernel(q_ref, k_ref, v_ref, z_ref, ba_ref, cwq_ref, cwk_ref, cwv_ref, alog_ref, dt_ref, og_ref,
               y_ref,
               xp, qn, kn, vn, aq_s, b_s, egl_s, o_s, *, seq, tr, ppb):
    head = pl.program_id(1)
    n_pairs = seq // PAIR
    n_tiles = seq // tr

    def conv_phase(x_ref, cw_ref, dst, normalise, scale):
        xp[0:8, :] = jnp.zeros((8, LANES), F32)
        xp[seq + 8:seq + 16, :] = jnp.zeros((8, LANES), F32)

        def copy_tile(t, c):
            r0 = pl.multiple_of(t * tr, tr)
            xp[pl.ds(r0 + 8, tr), :] = x_ref[0, pl.ds(r0, tr), :]
            return c

        lax.fori_loop(0, n_tiles, copy_tile, 0)
        cw = cw_ref[...]

        def tile(t, c):
            r0 = pl.multiple_of(t * tr, tr)
            win = xp[pl.ds(r0, tr + 16), :]
            acc = win[6:6 + tr] * cw[0:1]
            for kk in range(1, CONV_K):
                acc = acc + win[6 + kk:6 + kk + tr] * cw[kk:kk + 1]
            y = acc * _sigmoid(acc)
            if normalise:
                y = y * lax.rsqrt(jnp.sum(y * y, axis=-1, keepdims=True) + EPS)
            if scale != 1.0:
                y = y * scale
            dst[pl.ds(r0, tr), :] = y
            return c

        lax.fori_loop(0, n_tiles, tile, 0)

    conv_phase(q_ref, cwq_ref, qn, True, DN_HEAD_DIM ** -0.5)
    conv_phase(k_ref, cwk_ref, kn, True, 1.0)
    conv_phase(v_ref, cwv_ref, vn, False, 1.0)

    ri = lax.broadcasted_iota(I32, (PAIR, PAIR), 0)
    ci = lax.broadcasted_iota(I32, (PAIR, PAIR), 1)
    same = (ri // CHUNK) == (ci // CHUNK)
    mask_incl = (same & (ci <= ri), same & (ci >= ri))
    mask_strict = (same & (ci < ri), same & (ci > ri))
    eye = jnp.where(ri == ci, 1.0, 0.0).astype(F32)
    first_chunk = ci < CHUNK
    lu = jnp.concatenate([jnp.where(mask_incl[0], 1.0, 0.0), jnp.where(mask_incl[1], 1.0, 0.0)],
                         axis=0).astype(BF16)
    lu3 = jnp.concatenate([lu, lu, lu], axis=1)
    r2 = lax.broadcasted_iota(I32, (LANES, 2 * LANES), 0)
    c2 = lax.broadcasted_iota(I32, (LANES, 2 * LANES), 1)

    def sel2(col_a, col_b):
        return jnp.where(r2 == jnp.where(c2 < LANES, col_a, col_b), 1.0, 0.0).astype(BF16)

    e_beta = sel2(head, DN_HEADS + head)
    e_gate = sel2(2 * DN_HEADS + head, 3 * DN_HEADS + head)
    alog_row = alog_ref[...]
    dt_row = dt_ref[...]

    def replicate_parts(x, e):
        return _dot(jnp.concatenate(_split3(x), axis=0), e)

    def sum_parts(r):
        return r[:PAIR] + r[PAIR:2 * PAIR] + r[2 * PAIR:]

    def wide(x):
        return jnp.concatenate([x, x], axis=1)

    same16 = wide((ri // 16) == (ci // 16))
    same32 = wide((ri // 32) == (ci // 32))
    off16 = same32 & jnp.logical_not(same16)
    eye_w = wide(eye)
    zero_blk = jnp.zeros((PAIR, PAIR), BF16)

    def bdiag(w):
        return jnp.concatenate([jnp.concatenate([w[:, :PAIR], zero_blk], axis=1),
                                jnp.concatenate([zero_blk, w[:, PAIR:]], axis=1)], axis=0)

    def tri_inverse_many(lws):
        d0s = [jnp.where(same16, lw, 0.0) for lw in lws]
        ms = [(-d0).astype(BF16) for d0 in d0s]
        xs = [eye_w - d0 for d0 in d0s]
        for _ in range(3):
            ms = [_dot(m, bdiag(m)).astype(BF16) for m in ms]
            xs = [x + _dot(x.astype(BF16), bdiag(m)) for x, m in zip(xs, ms)]
        for level in range(2):
            cs = [bdiag((jnp.where(off16, lw, 0.0) if level == 0 else jnp.where(same32, 0.0, lw)).astype(BF16))
                  for lw in lws]
            xbs = [x.astype(BF16) for x in xs]
            ts = [_dot(xb, c).astype(BF16) for xb, c in zip(xbs, cs)]
            xs = [x - _dot(t, bdiag(xb)) for x, t, xb in zip(xs, ts, xbs)]
        rs = []
        for lw, x in zip(lws, xs):
            lh, ll = _split2(lw)
            xh, xl = _split2(x)
            prod = _dot(jnp.concatenate([lh, lh, ll], axis=1),
                        jnp.concatenate([bdiag(xh), bdiag(xl), bdiag(xh)], axis=0))
            rs.append((eye_w - x) - prod)
        return [x + _dot(x.astype(BF16), bdiag(r.astype(BF16))) for x, r in zip(xs, rs)]

    def gate_stage(pairs):
        r0s = [pl.multiple_of(p * PAIR, PAIR) for p in pairs]
        blks = [ba_ref[0, pl.ds(r0, PAIR), :] for r0 in r0s]
        gs = [-jnp.exp(alog_row) * _softplus(blk + dt_row) for blk in blks]
        betas = [replicate_parts(_sigmoid(blk), e_beta) for blk in blks]
        gps = [replicate_parts(g, e_gate) for g in gs]
        kqs = []
        for r0 in r0s:
            kb = kn[pl.ds(r0, PAIR), :].astype(BF16)
            kqs.append(_dot_nt(jnp.concatenate([kb, qn[pl.ds(r0, PAIR), :].astype(BF16)], axis=0), kb))
        css = [_dot(lu3, gp.astype(BF16)) for gp in gps]
        out = []
        for r0, beta3, gp, cs, kq in zip(r0s, betas, gps, css, kqs):
            beta, g_rep = sum_parts(beta3), sum_parts(gp)
            pre, suf = cs[:PAIR], cs[PAIR:]
            per_dir = ((beta[:, :LANES], pre[:, :LANES], suf[:, :LANES] - g_rep[:, :LANES]),
                       (beta[:, LANES:], suf[:, LANES:], pre[:, LANES:] - g_rep[:, LANES:]))
            out.append((r0, per_dir, kq[:PAIR], kq[PAIR:]))
        return out

    def finish_stage(chains, t_invs):
        egs, wus = [], []
        for (p, r0, d, beta, gc, ex, qk, dec), t_inv in zip(chains, t_invs):
            k2 = kn[pl.ds(r0, PAIR), :]
            eg = jnp.exp(gc)
            vb = (vn[pl.ds(r0, PAIR), :] * beta).astype(BF16)
            kbg = (k2 * beta * eg).astype(BF16)
            egs.append(eg)
            wus.append(_dot(t_inv.astype(BF16), jnp.concatenate([kbg, vb], axis=1)).astype(BF16))
        abs_, qos = [], []
        for (p, r0, d, beta, gc, ex, qk, dec), wu in zip(chains, wus):
            kst = (kn[pl.ds(r0, PAIR), :] * jnp.exp(ex)).T.astype(BF16)
            kst2 = jnp.concatenate([jnp.where(first_chunk, kst, 0.0), jnp.where(first_chunk, 0.0, kst)],
                                   axis=0).astype(BF16)
            abs_.append(_dot(kst2, wu))
            intra = jnp.where(mask_incl[d], qk * dec, 0.0).astype(BF16)
            qos.append(_dot(intra, wu))
        for (p, r0, d, beta, gc, ex, qk, dec), eg, ab, qo in zip(chains, egs, abs_, qos):
            store_chain(p, r0, d, gc, ex, qn[pl.ds(r0, PAIR), :] * eg, ab, qo)

    def store_chain(p, r0, d, gc, ex, qd, ab, qo):
        a0 = pl.multiple_of(p * (2 * AQ_ROWS), 2 * AQ_ROWS)
        for sub in range(2):
            rows = slice(sub * CHUNK, (sub + 1) * CHUNK)
            base = a0 + sub * AQ_ROWS
            aq_s[d, pl.ds(base, PAIR), :] = (-ab[sub * PAIR:(sub + 1) * PAIR, :LANES]).astype(BF16)
            aq_s[d, pl.ds(base + PAIR, CHUNK), :] = (qd[rows] - qo[rows, :LANES]).astype(BF16)
        b_s[d, pl.ds(pl.multiple_of(p * 2 * PAIR, 2 * PAIR), 2 * PAIR), :] = ab[:, LANES:]
        o_s[pl.ds(r0, PAIR), :] = o_s[pl.ds(r0, PAIR), :] + qo[:, LANES:]
        etot = jnp.exp(gc + ex)
        e0 = pl.multiple_of(p * 16, 16)
        egl_s[d, pl.ds(e0, 8), :] = etot[0:8]
        egl_s[d, pl.ds(e0 + 8, 8), :] = etot[CHUNK:CHUNK + 8]

    def prep_block(i, c):
        pairs = [i * ppb + j for j in range(ppb)]
        chains, lws = [], []
        for p, (r0, per_dir, kk, qk) in zip(pairs, gate_stage(pairs)):
            lms = []
            for d in range(2):
                beta, gc, ex = per_dir[d]
                m_in = mask_incl[d]
                dec = jnp.where(m_in, jnp.exp(jnp.where(m_in, gc - gc.T, 0.0)), 0.0)
                lms.append(jnp.where(mask_strict[d], beta * kk * dec, 0.0))
                chains.append((p, r0, d, beta, gc, ex, qk, dec))
            lws.append(jnp.concatenate(lms, axis=1))
        t_invs = []
        for t_w in tri_inverse_many(lws):
            t_invs += [t_w[:, :PAIR], t_w[:, PAIR:]]
        finish_stage(chains, t_invs)
        return c

    def zero_tile(t, c):
        r0 = pl.multiple_of(t * tr, tr)
        o_s[pl.ds(r0, tr), :] = jnp.zeros((tr, LANES), F32)
        return c

    lax.fori_loop(0, n_tiles, zero_tile, 0)
    lax.fori_loop(0, n_pairs // ppb, prep_block, 0)

    def chunk_step(d, pair, sub, state):
        cidx = pair * 2 + sub
        r0 = pl.multiple_of(cidx * CHUNK, CHUNK)
        a0 = pl.multiple_of(cidx * AQ_ROWS, CHUNK)
        res = _dot(aq_s[d, pl.ds(a0, AQ_ROWS), :], state.astype(BF16))
        o_s[pl.ds(r0, CHUNK), :] = o_s[pl.ds(r0, CHUNK), :] + res[PAIR:]
        e0 = pl.multiple_of(cidx * 8, 8)
        eg = egl_s[d, pl.ds(e0, 8), :][0:1, :]
        return state * eg + res[:PAIR] + b_s[d, pl.ds(pl.multiple_of(cidx * PAIR, PAIR), PAIR), :]

    def pair_scan(n, carry):
        s_f, s_b = carry
        pb = n_pairs - 1 - n
        s_f = chunk_step(0, n, 0, s_f)
        s_b = chunk_step(1, pb, 1, s_b)
        s_f = chunk_step(0, n, 1, s_f)
        s_b = chunk_step(1, pb, 0, s_b)
        return s_f, s_b

    zero_state = jnp.zeros((DN_HEAD_DIM, DN_HEAD_DIM), F32)
    lax.fori_loop(0, n_pairs, pair_scan, (zero_state, zero_state))

    og = og_ref[...]

    def out_tile(t, c):
        r0 = pl.multiple_of(t * tr, tr)
        o = o_s[pl.ds(r0, tr), :]
        z = z_ref[0, pl.ds(r0, tr), :]
        y = o * lax.rsqrt(jnp.mean(o * o, axis=-1, keepdims=True) + EPS) * og
        y_ref[0, pl.ds(r0, tr), :] = (y * (z * _sigmoid(z))).astype(BF16)
        return c

    lax.fori_loop(0, n_tiles, out_tile, 0)


def _dn_call(proj, conv_w, alog_row, dt_row, o_gain):
    b, s, _ = proj.shape
    hd = DN_HEAD_DIM
    tr = min(s, 256)
    qb, kb, vb, zb = COL_Q // hd, COL_K // hd, COL_V // hd, COL_Z // hd
    bab = COL_BA // LANES

    def col(base):
        return pl.BlockSpec((1, s, hd), lambda bi, h: (bi, 0, base + h))

    def cw(base):
        return pl.BlockSpec((CONV_K, hd), lambda bi, h: (0, base + h))

    row = pl.BlockSpec((1, LANES), lambda bi, h: (0, 0))
    n_pairs = s // PAIR
    ppb = next(n for n in (8, 4, 2, 1) if n_pairs % n == 0)
    kern = functools.partial(_dn_kernel, seq=s, tr=tr, ppb=ppb)
    return pl.pallas_call(
        kern,
        grid=(b, DN_HEADS),
        in_specs=[col(qb), col(kb), col(vb), col(zb),
                  pl.BlockSpec((1, s, LANES), lambda bi, h: (bi, 0, bab)),
                  cw(qb), cw(kb), cw(vb), row, row, row],
        out_specs=pl.BlockSpec((1, s, hd), lambda bi, h: (bi, 0, h)),
        out_shape=jax.ShapeDtypeStruct((b, s, DN_HEADS * hd), BF16),
        scratch_shapes=[pltpu.VMEM((s + 16, LANES), F32),
                        pltpu.VMEM((s, hd), F32), pltpu.VMEM((s, hd), F32), pltpu.VMEM((s, hd), F32),
                        pltpu.VMEM((2, s // CHUNK * AQ_ROWS, hd), BF16),
                        pltpu.VMEM((2, s // CHUNK * hd, hd), F32),
                        pltpu.VMEM((2, s // CHUNK * 8, LANES), F32),
                        pltpu.VMEM((s, hd), F32)],
        compiler_params=_cparams(("parallel", "parallel")),
    )(proj, proj, proj, proj, proj, conv_w, conv_w, conv_w, alog_row, dt_row, o_gain.reshape(1, hd))


def _mlaproj_kernel(cq_ref, ckv_ref, kr_ref, pos_ref, invf_ref, sgn_ref, qg_ref, kvg_ref,
                    wqn_ref, wqr_ref, wqs_ref, wkn_ref, wv_ref,
                    q_ref, k_ref, v_ref):
    cq = cq_ref[0]
    hq = (cq * lax.rsqrt(jnp.mean(cq * cq, axis=-1, keepdims=True) + EPS) * qg_ref[...]).astype(BF16)
    qn = _dot(hq, wqn_ref[...])
    qr = _dot(hq, wqr_ref[...])
    qs = _dot(hq, wqs_ref[...])
    ang = pos_ref[0].astype(F32) * invf_ref[...]
    cc = jnp.cos(ang)
    ss = jnp.sin(ang) * sgn_ref[...]
    scale = QK_DIM ** -0.5 * math.log2(math.e)
    ckv = ckv_ref[0]
    hkv = (ckv * lax.rsqrt(jnp.mean(ckv * ckv, axis=-1, keepdims=True) + EPS) * kvg_ref[...]).astype(BF16)
    kn = _dot(hkv, wkn_ref[...])
    vv = _dot(hkv, wv_ref[...])
    kr = kr_ref[0]
    k_rope = (kr[:, :ROPE_DIM] * cc + kr[:, ROPE_DIM:] * ss).astype(BF16)
    for h in range(MLA_HEADS):
        lo, hi = h * ROPE_DIM, (h + 1) * ROPE_DIM
        q_rope = qr[:, lo:hi] * cc + qs[:, lo:hi] * ss
        q_ref[0, h, :, 0:NOPE_DIM] = (qn[:, h * NOPE_DIM:(h + 1) * NOPE_DIM] * scale).astype(BF16)
        q_ref[0, h, :, NOPE_DIM:QK_DIM] = (q_rope * scale).astype(BF16)
        k_ref[0, h, :, 0:NOPE_DIM] = kn[:, h * NOPE_DIM:(h + 1) * NOPE_DIM].astype(BF16)
        k_ref[0, h, :, NOPE_DIM:QK_DIM] = k_rope
        v_ref[0, h] = vv[:, h * V_DIM:(h + 1) * V_DIM].astype(BF16)


def _mlaproj_call(proj, pos3, invf, sgn, q_gain, kv_gain, wqn, wqr, wqs, wkn, wv):
    b, s, _ = proj.shape
    tm = min(s, 512)
    nh = MLA_HEADS

    def full(a):
        return pl.BlockSpec(a.shape, lambda bi, i: (0,) * a.ndim)

    qg = q_gain.reshape(1, Q_LORA)
    kvg = kv_gain.reshape(1, KV_LORA)
    return pl.pallas_call(
        _mlaproj_kernel,
        grid=(b, s // tm),
        in_specs=[pl.BlockSpec((1, tm, Q_LORA), lambda bi, i: (bi, i, COL_CQ // Q_LORA)),
                  pl.BlockSpec((1, tm, KV_LORA), lambda bi, i: (bi, i, COL_CKV // KV_LORA)),
                  pl.BlockSpec((1, tm, LANES), lambda bi, i: (bi, i, COL_KR // LANES)),
                  pl.BlockSpec((1, tm, 1), lambda bi, i: (bi, i, 0)),
                  full(invf), full(sgn), full(qg), full(kvg),
                  full(wqn), full(wqr), full(wqs), full(wkn), full(wv)],
        out_specs=[pl.BlockSpec((1, nh, tm, QK_DIM), lambda bi, i: (bi, 0, i, 0)),
                   pl.BlockSpec((1, nh, tm, QK_DIM), lambda bi, i: (bi, 0, i, 0)),
                   pl.BlockSpec((1, nh, tm, V_DIM), lambda bi, i: (bi, 0, i, 0))],
        out_shape=[jax.ShapeDtypeStruct((b, nh, s, QK_DIM), BF16),
                   jax.ShapeDtypeStruct((b, nh, s, QK_DIM), BF16),
                   jax.ShapeDtypeStruct((b, nh, s, V_DIM), BF16)],
        compiler_params=_cparams(("parallel", "parallel")),
    )(proj, proj, proj, pos3, invf, sgn, qg, kvg, wqn, wqr, wqs, wkn, wv)


def _attn_kernel(q_ref, k_ref, v_ref, o_ref, *, seq, tq, sub, tk):
    n_sub, n_c = tq // sub, seq // tk

    def scores(a, c):
        return _dot_nt(q_ref[0, 0, a * sub:(a + 1) * sub, :], k_ref[0, 0, c * tk:(c + 1) * tk, :])

    s_cur = [scores(0, c) for c in range(n_c)]
    for a in range(n_sub):
        mx = s_cur[0]
        for c in range(1, n_c):
            mx = jnp.maximum(mx, s_cur[c])
        m = jnp.max(mx, axis=-1, keepdims=True)
        lp = jnp.zeros((sub, LANES), F32)
        acc = jnp.zeros((sub, V_DIM), F32)
        s_next = []
        for c in range(n_c):
            p = jnp.exp2(s_cur[c] - m)
            for j in range(tk // LANES):
                lp = lp + p[:, j * LANES:(j + 1) * LANES]
            acc = acc + _dot(p.astype(BF16), v_ref[0, 0, c * tk:(c + 1) * tk, :])
            if a + 1 < n_sub:
                s_next.append(scores(a + 1, c))
        l = jnp.sum(lp, axis=-1, keepdims=True)
        o_ref[0, a * sub:(a + 1) * sub, :] = (acc / l).astype(BF16)
        s_cur = s_next


def _attn_call(q, k, v):
    b, nh, s, _ = q.shape
    tq = min(s, 1024)
    kern = functools.partial(_attn_kernel, seq=s, tq=tq, sub=min(tq, 256), tk=min(s, 512))
    return pl.pallas_call(
        kern,
        grid=(b, nh, s // tq),
        in_specs=[pl.BlockSpec((1, 1, tq, QK_DIM), lambda bi, h, i: (bi, h, i, 0)),
                  pl.BlockSpec((1, 1, s, QK_DIM), lambda bi, h, i: (bi, h, 0, 0)),
                  pl.BlockSpec((1, 1, s, V_DIM), lambda bi, h, i: (bi, h, 0, 0))],
        out_specs=pl.BlockSpec((1, tq, V_DIM), lambda bi, h, i: (bi, i, h)),
        out_shape=jax.ShapeDtypeStruct((b, s, nh * V_DIM), BF16),
        compiler_params=_cparams(("parallel", "parallel", "arbitrary")),
    )(q, k, v)


def _merge_kernel(ydn_ref, ymla_ref, gdn_ref, gmla_ref, x_ref, mod_ref, wodn_ref, womla_ref, wout_ref,
                  gffn_ref, wrt_ref,
                  x1_ref, h2_ref, aff_ref):
    y_dn = _dot(ydn_ref[0], wodn_ref[...])
    y_mla = _dot(ymla_ref[0], womla_ref[...])
    merged = _sigmoid(gdn_ref[0]) * y_dn + _sigmoid(gmla_ref[0]) * y_mla
    x1 = x_ref[0] + mod_ref[0, 2:3, :] * _dot(merged.astype(BF16), wout_ref[...])
    x1_ref[0] = x1
    y = x1 * lax.rsqrt(jnp.mean(x1 * x1, axis=-1, keepdims=True) + EPS) * gffn_ref[...]
    h2 = (y * (1.0 + mod_ref[0, 4:5, :]) + mod_ref[0, 3:4, :]).astype(BF16)
    h2_ref[0] = h2
    logits = _dot_nt(wrt_ref[...], h2)
    mx = jnp.max(logits, axis=0, keepdims=True)
    ex = jnp.exp(logits - mx)
    aff_ref[0] = ex / jnp.sum(ex, axis=0, keepdims=True)


def _merge_call(y_dn, y_mla, proj, x, mod3, w_o_dn, w_o_mla, w_out, g_ffn, w_router_t):
    b, s, d = x.shape
    tm = min(s, 512)
    ne = w_router_t.shape[0]

    def tok(width):
        return pl.BlockSpec((1, tm, width), lambda bi, i: (bi, i, 0))

    def full(a):
        return pl.BlockSpec(a.shape, lambda bi, i: (0,) * a.ndim)

    gf = g_ffn.reshape(1, d)
    return pl.pallas_call(
        _merge_kernel,
        grid=(b, s // tm),
        in_specs=[tok(d), tok(d),
                  pl.BlockSpec((1, tm, d), lambda bi, i: (bi, i, COL_GATE // d)),
                  pl.BlockSpec((1, tm, d), lambda bi, i: (bi, i, COL_GATE // d + 1)),
                  tok(d),
                  pl.BlockSpec((1, 6, d), lambda bi, i: (bi, 0, 0)),
                  full(w_o_dn), full(w_o_mla), full(w_out), full(gf), full(w_router_t)],
        out_specs=[tok(d), tok(d), pl.BlockSpec((1, ne, tm), lambda bi, i: (bi, 0, i))],
        out_shape=[jax.ShapeDtypeStruct((b, s, d), F32),
                   jax.ShapeDtypeStruct((b, s, d), BF16),
                   jax.ShapeDtypeStruct((b, ne, s), F32)],
        compiler_params=_cparams(("parallel", "parallel")),
    )(y_dn, y_mla, proj, proj, x, mod3, w_o_dn, w_o_mla, w_out, gf, w_router_t)


def _route_kernel(aff_ref, pos_ref, *, seq, cap):
    a = aff_ref[0]
    ne = a.shape[0]
    capf = jnp.float32(cap)

    def count_ge(t_bits):
        t = pltpu.bitcast(t_bits, F32)
        return jnp.sum(jnp.where(a >= t, 1.0, 0.0), axis=1, keepdims=True)

    def search(_, carry):
        lo, hi = carry
        mid = lo + ((hi - lo + 1) >> 1)
        ok = count_ge(mid) >= capf
        return jnp.where(ok, mid, lo), jnp.where(ok, hi, mid - 1)

    lo0 = jnp.zeros((ne, 1), I32)
    hi0 = jnp.full((ne, 1), 0x7F800000, I32)
    thr_bits, _ = lax.fori_loop(0, 32, search, (lo0, hi0))
    thr = pltpu.bitcast(thr_bits, F32)

    ur = lax.broadcasted_iota(I32, (LANES, LANES), 0)
    uc = lax.broadcasted_iota(I32, (LANES, LANES), 1)
    upper = jnp.where(ur <= uc, 1.0, 0.0).astype(BF16)

    def prefix_incl(mask):
        carry = jnp.zeros((ne, 1), F32)
        outs = []
        for j in range(seq // LANES):
            pr = _dot(mask[:, j * LANES:(j + 1) * LANES].astype(BF16), upper) + carry
            outs.append(pr)
            carry = pr[:, LANES - 1:LANES]
        return jnp.concatenate(outs, axis=1)

    gt = a > thr
    eq = a == thr
    n_gt = jnp.sum(jnp.where(gt, 1.0, 0.0), axis=1, keepdims=True)
    take_eq = eq & (prefix_incl(jnp.where(eq, 1.0, 0.0)) <= capf - n_gt)
    sel = gt | take_eq
    slot = prefix_incl(jnp.where(sel, 1.0, 0.0)) - 1.0
    pos_ref[0] = jnp.where(sel, slot, -1.0).astype(I32)


def _route_call(aff, cap):
    b, ne, s = aff.shape
    kern = functools.partial(_route_kernel, seq=s, cap=cap)
    return pl.pallas_call(
        kern,
        grid=(b,),
        in_specs=[pl.BlockSpec((1, ne, s), lambda bi: (bi, 0, 0))],
        out_specs=pl.BlockSpec((1, ne, s), lambda bi: (bi, 0, 0)),
        out_shape=jax.ShapeDtypeStruct((b, ne, s), I32),
        compiler_params=_cparams(("parallel",)),
    )(aff)


def _gather_kernel(pos_ref, aff_ref, h_ref, xe_ref, gate_ref, *, seq, cap, tk):
    slot = lax.broadcasted_iota(I32, (cap, 1), 0)
    acc = jnp.zeros((cap, h_ref.shape[-1]), F32)
    gacc = jnp.zeros((cap, 1), F32)
    for kt in range(seq // tk):
        pm = pos_ref[0, 0, :, kt * tk:(kt + 1) * tk]
        hit = pm == slot
        acc = acc + _dot(jnp.where(hit, 1.0, 0.0).astype(BF16), h_ref[0, kt * tk:(kt + 1) * tk, :])
        am = aff_ref[0, 0, :, kt * tk:(kt + 1) * tk]
        gacc = gacc + jnp.sum(jnp.where(hit, am, 0.0), axis=1, keepdims=True)
    xe_ref[0, 0] = acc.astype(BF16)
    gate_ref[0, 0] = jnp.broadcast_to(gacc, (cap, LANES))


def _gather_call(pos4, aff4, h2, cap):
    b, ne, _, s = pos4.shape
    d = h2.shape[-1]
    tk = min(s, 512)
    kern = functools.partial(_gather_kernel, seq=s, cap=cap, tk=tk)
    return pl.pallas_call(
        kern,
        grid=(b, ne),
        in_specs=[pl.BlockSpec((1, 1, 1, s), lambda bi, e: (bi, e, 0, 0)),
                  pl.BlockSpec((1, 1, 1, s), lambda bi, e: (bi, e, 0, 0)),
                  pl.BlockSpec((1, s, d), lambda bi, e: (bi, 0, 0))],
        out_specs=[pl.BlockSpec((1, 1, cap, d), lambda bi, e: (bi, e, 0, 0)),
                   pl.BlockSpec((1, 1, cap, LANES), lambda bi, e: (bi, e, 0, 0))],
        out_shape=[jax.ShapeDtypeStruct((b, ne, cap, d), BF16),
                   jax.ShapeDtypeStruct((b, ne, cap, LANES), F32)],
        compiler_params=_cparams(("parallel", "arbitrary")),
    )(pos4, aff4, h2)


def _ffn_kernel(xe_ref, gate_ref, wg_ref, wu_ref, wd_ref, ye_ref):
    x = xe_ref[0, 0]
    g = _dot(x, wg_ref[0])
    u = _dot(x, wu_ref[0])
    hid = (g * _sigmoid(g)) * u
    y = _dot(hid.astype(BF16), wd_ref[0])
    ye_ref[0, 0] = (y * gate_ref[0, 0][:, 0:1]).astype(BF16)


def _ffn_call(xe, gate, wg, wu, wd):
    b, ne, cap, d = xe.shape
    f = wg.shape[-1]
    return pl.pallas_call(
        _ffn_kernel,
        grid=(ne, b),
        in_specs=[pl.BlockSpec((1, 1, cap, d), lambda e, bi: (bi, e, 0, 0)),
                  pl.BlockSpec((1, 1, cap, LANES), lambda e, bi: (bi, e, 0, 0)),
                  pl.BlockSpec((1, d, f), lambda e, bi: (e, 0, 0)),
                  pl.BlockSpec((1, d, f), lambda e, bi: (e, 0, 0)),
                  pl.BlockSpec((1, f, d), lambda e, bi: (e, 0, 0))],
        out_specs=pl.BlockSpec((1, 1, cap, d), lambda e, bi: (bi, e, 0, 0)),
        out_shape=jax.ShapeDtypeStruct((b, ne, cap, d), BF16),
        compiler_params=_cparams(("parallel", "arbitrary")),
    )(xe, gate, wg, wu, wd)


def _combine_kernel(pos_ref, ye_ref, x1_ref, mod_ref, gfin_ref, o_ref, acc_ref, *, cap, last_norm):
    g = pl.program_id(2)
    eg = ye_ref.shape[1]

    @pl.when(g == 0)
    def _():
        acc_ref[...] = jnp.zeros_like(acc_ref)

    slot = lax.broadcasted_iota(I32, (1, cap), 1)
    pos = pos_ref[0, 0]
    onehot = jnp.concatenate(
        [jnp.where(pos[:, j:j + 1] == slot, 1.0, 0.0).astype(BF16) for j in range(eg)], axis=1)
    acc_ref[...] += _dot(onehot, ye_ref[0].reshape(eg * cap, ye_ref.shape[-1]))

    @pl.when(g == pl.num_programs(2) - 1)
    def _():
        x2 = x1_ref[0] + mod_ref[0, 5:6, :] * acc_ref[...]
        if last_norm:
            x2 = x2 * lax.rsqrt(jnp.mean(x2 * x2, axis=-1, keepdims=True) + EPS) * gfin_ref[...]
        o_ref[0] = x2


def _combine_call(pos_tok, ye, x1, mod3, g_final, cap, last_norm):
    b, s, d = x1.shape
    ne = ye.shape[1]
    n_groups, eg = pos_tok.shape[1], pos_tok.shape[3]
    tm = min(s, 512)
    kern = functools.partial(_combine_kernel, cap=cap, last_norm=last_norm)
    return pl.pallas_call(
        kern,
        grid=(b, s // tm, n_groups),
        in_specs=[pl.BlockSpec((1, 1, tm, eg), lambda bi, i, e: (bi, e, i, 0)),
                  pl.BlockSpec((1, eg, cap, d), lambda bi, i, e: (bi, e, 0, 0)),
                  pl.BlockSpec((1, tm, d), lambda bi, i, e: (bi, i, 0)),
                  pl.BlockSpec((1, 6, d), lambda bi, i, e: (bi, 0, 0)),
                  pl.BlockSpec((1, d), lambda bi, i, e: (0, 0))],
        out_specs=pl.BlockSpec((1, tm, d), lambda bi, i, e: (bi, i, 0)),
        out_shape=jax.ShapeDtypeStruct((b, s, d), F32),
        scratch_shapes=[pltpu.VMEM((tm, d), F32)],
        compiler_params=_cparams(("parallel", "parallel", "arbitrary")),
    )(pos_tok, ye, x1, mod3, g_final.reshape(1, d))


def _pack_w_in(w):
    o_ba = 4 * 1024 + 0
    o_b = o_ba
    o_a = o_b + 2 * DN_HEADS
    o_cq = o_a + 2 * DN_HEADS
    o_ckv = o_cq + Q_LORA
    o_kr = o_ckv + KV_LORA
    o_g = o_kr + ROPE_DIM
    half = ROPE_DIM // 2
    w_kr = w[:, o_kr:o_g]
    w_kr_sw = jnp.concatenate([w_kr[:, half:], w_kr[:, :half]], axis=1)
    pad = jnp.zeros((w.shape[0], COL_GATE - COL_BA - 4 * DN_HEADS), w.dtype)
    packed = jnp.concatenate([w[:, :o_ba], w[:, o_cq:o_ckv], w[:, o_ckv:o_kr], w_kr, w_kr_sw,
                              w[:, o_b:o_a], w[:, o_a:o_cq], pad, w[:, o_g:]], axis=1)
    assert packed.shape[1] == N_PAD
    return packed.astype(BF16)


def _lane_row(vals, offset):
    row = jnp.zeros((1, LANES), F32)
    return row.at[0, offset:offset + vals.size].set(vals.reshape(-1).astype(F32))


def kernel(x, c, positions, w_mod, b_mod, g_mix, w_in, conv_w, a_log, dt_bias, dn_o_gain, q_gain, w_uq,
           kv_gain, w_ukv, w_o_dn, w_o_mla, w_out, g_ffn, w_router, w_gate, w_up, w_down, g_final):
    b, s, d = x.shape
    depth = w_mod.shape[0]
    cap = CAPACITY_FACTOR * s // N_EXPERTS
    half = ROPE_DIM // 2
    inv_freq = ROPE_THETA ** (-jnp.arange(half, dtype=F32) / half)
    invf = jnp.concatenate([inv_freq, inv_freq]).reshape(1, ROPE_DIM)
    sgn = jnp.concatenate([-jnp.ones((half,), F32), jnp.ones((half,), F32)]).reshape(1, ROPE_DIM)
    pos3 = positions.reshape(b, s, 1)
    c_pad = jnp.zeros((8, d), F32).at[:b].set(c)
    swap = np.concatenate([np.arange(half, ROPE_DIM), np.arange(half)])

    for l in range(depth):
        mod3 = _mod_call(c_pad, w_mod[l], b_mod[l])[:b].reshape(b, 6, d)
        proj = _inproj_call(x, mod3, g_mix[l], _pack_w_in(w_in[l]))

        alog_row = _lane_row(a_log[l], 2 * DN_HEADS)
        dt_row = _lane_row(dt_bias[l], 2 * DN_HEADS)
        y_dn = _dn_call(proj, conv_w[l], alog_row, dt_row, dn_o_gain[l])

        wq = w_uq[l].reshape(Q_LORA, MLA_HEADS, QK_DIM)
        wqn = wq[:, :, :NOPE_DIM].reshape(Q_LORA, MLA_HEADS * NOPE_DIM).astype(BF16)
        wqr = wq[:, :, NOPE_DIM:].reshape(Q_LORA, MLA_HEADS * ROPE_DIM).astype(BF16)
        wqs = wq[:, :, NOPE_DIM:][:, :, swap].reshape(Q_LORA, MLA_HEADS * ROPE_DIM).astype(BF16)
        wkv = w_ukv[l].reshape(KV_LORA, MLA_HEADS, NOPE_DIM + V_DIM)
        wkn = wkv[:, :, :NOPE_DIM].reshape(KV_LORA, MLA_HEADS * NOPE_DIM).astype(BF16)
        wv = wkv[:, :, NOPE_DIM:].reshape(KV_LORA, MLA_HEADS * V_DIM).astype(BF16)
        q, k, v = _mlaproj_call(proj, pos3, invf, sgn, q_gain[l], kv_gain[l], wqn, wqr, wqs, wkn, wv)
        y_mla = _attn_call(q, k, v)

        x1, h2, aff = _merge_call(y_dn, y_mla, proj, x, mod3, w_o_dn[l].astype(BF16),
                                  w_o_mla[l].astype(BF16), w_out[l].astype(BF16), g_ffn[l],
                                  w_router[l].T.astype(BF16))
        pos = _route_call(aff, cap)
        pos4 = pos.reshape(b, N_EXPERTS, 1, s)
        aff4 = aff.reshape(b, N_EXPERTS, 1, s)
        xe, gate = _gather_call(pos4, aff4, h2, cap)
        ye = _ffn_call(xe, gate, w_gate[l].astype(BF16), w_up[l].astype(BF16), w_down[l].astype(BF16))
        pos_tok = pos.reshape(b, N_EXPERTS // COMBINE_GROUP, COMBINE_GROUP, s).transpose(0, 1, 3, 2)
        x = _combine_call(pos_tok, ye, x1, mod3, g_final, cap, last_norm=(l == depth - 1))
    return x
```

```python
import functools
import math

import jax
import jax.numpy as jnp
import numpy as np
from jax import lax
from jax.experimental import pallas as pl
from jax.experimental.pallas import tpu as pltpu

F32 = jnp.float32
BF16 = jnp.bfloat16
I32 = jnp.int32

EPS = 1e-6
DN_HEADS = 8
DN_HEAD_DIM = 128
CONV_K = 5
CHUNK = 64
PAIR = 2 * CHUNK
AQ_ROWS = DN_HEAD_DIM + PAIR
MLA_HEADS = 8
NOPE_DIM = 128
ROPE_DIM = 64
V_DIM = 128
QK_DIM = NOPE_DIM + ROPE_DIM
Q_LORA = 512
KV_LORA = 256
ROPE_THETA = 10000.0
N_EXPERTS = 16
CAPACITY_FACTOR = 2
COMBINE_GROUP = 4
LANES = 128
VMEM_LIMIT = 56 * 1024 * 1024

COL_Q, COL_K, COL_V, COL_Z = 0, 1024, 2048, 3072
COL_CQ, COL_CKV, COL_KR, COL_BA, COL_GATE = 4096, 4608, 4864, 4992, 5120
N_PAD = 7168


def _cparams(sem):
    return pltpu.CompilerParams(dimension_semantics=sem, vmem_limit_bytes=VMEM_LIMIT)


def _dot(a, b):
    return jnp.dot(a, b, preferred_element_type=F32)


def _dot_nt(a, b):
    return lax.dot_general(a, b, (((1,), (1,)), ((), ())), preferred_element_type=F32)


def _dot_tn(a, b):
    return lax.dot_general(a, b, (((0,), (0,)), ((), ())), preferred_element_type=F32)


def _sigmoid(x):
    return 1.0 / (1.0 + jnp.exp(-x))


def _softplus(x):
    return jnp.maximum(x, 0.0) + jnp.log(1.0 + jnp.exp(-jnp.abs(x)))


def _split2(x):
    h = x.astype(BF16)
    return h, (x - h.astype(F32)).astype(BF16)


def _split3(x):
    h = x.astype(BF16)
    r = x - h.astype(F32)
    m = r.astype(BF16)
    return h, m, (r - m.astype(F32)).astype(BF16)


def _dot3_parts(a_parts, b_parts):
    ah, al = a_parts
    bh, bl = b_parts
    return _dot(jnp.concatenate([ah, ah, al], axis=1), jnp.concatenate([bh, bl, bh], axis=0))


def _mod_kernel(c_ref, w_ref, b_ref, o_ref):
    c = c_ref[...]
    s = c * _sigmoid(c)
    o_ref[...] = _dot(s.astype(BF16), w_ref[...].astype(BF16)) + b_ref[...]


def _mod_call(c_pad, w_mod, b_mod):
    rows, d = c_pad.shape
    n = w_mod.shape[1]
    tn = 1024
    return pl.pallas_call(
        _mod_kernel,
        grid=(n // tn,),
        in_specs=[pl.BlockSpec((rows, d), lambda j: (0, 0)),
                  pl.BlockSpec((d, tn), lambda j: (0, j)),
                  pl.BlockSpec((1, tn), lambda j: (0, j))],
        out_specs=pl.BlockSpec((rows, tn), lambda j: (0, j)),
        out_shape=jax.ShapeDtypeStruct((rows, n), F32),
        compiler_params=_cparams(("arbitrary",)),
    )(c_pad, w_mod, b_mod.reshape(1, n))


def _inproj_kernel(x_ref, mod_ref, g_ref, w_ref, o_ref, h_scr):
    @pl.when(pl.program_id(2) == 0)
    def _():
        x = x_ref[0]
        ms = jnp.mean(x * x, axis=-1, keepdims=True)
        y = x * lax.rsqrt(ms + EPS) * g_ref[...]
        h = y * (1.0 + mod_ref[0, 1:2, :]) + mod_ref[0, 0:1, :]
        h_scr[...] = h.astype(BF16)

    o_ref[0] = _dot(h_scr[...], w_ref[...])


def _inproj_call(x, mod3, g_mix, w_in_p):
    b, s, d = x.shape
    n = w_in_p.shape[1]
    tm = min(s, 1024)
    tn = 1024
    return pl.pallas_call(
        _inproj_kernel,
        grid=(b, s // tm, n // tn),
        in_specs=[pl.BlockSpec((1, tm, d), lambda bi, i, j: (bi, i, 0)),
                  pl.BlockSpec((1, 6, d), lambda bi, i, j: (bi, 0, 0)),
                  pl.BlockSpec((1, d), lambda bi, i, j: (0, 0)),
                  pl.BlockSpec((d, tn), lambda bi, i, j: (0, j))],
        out_specs=pl.BlockSpec((1, tm, tn), lambda bi, i, j: (bi, i, j)),
        out_shape=jax.ShapeDtypeStruct((b, s, n), F32),
        scratch_shapes=[pltpu.VMEM((tm, d), BF16)],
        compiler_params=_cparams(("parallel", "parallel", "arbitrary")),
    )(x, mod3, g_mix.reshape(1, d), w_in_p)


def _dn_kernel(q_ref, k_ref, v_ref, z_ref, ba_ref, cwq_ref, cwk_ref, cwv_ref, alog_ref, dt_ref, og_ref,
               y_ref,
               xp, qn, kn, vn, aq_s, b_s, egl_s, o_s, *, seq, tr, ppb):
    head = pl.program_id(1)
    n_pairs = seq // PAIR
    n_tiles = seq // tr

    def conv_phase(x_ref, cw_ref, dst, normalise, scale):
        xp[0:8, :] = jnp.zeros((8, LANES), F32)
        xp[seq + 8:seq + 16, :] = jnp.zeros((8, LANES), F32)

        def copy_tile(t, c):
            r0 = pl.multiple_of(t * tr, tr)
            xp[pl.ds(r0 + 8, tr), :] = x_ref[0, pl.ds(r0, tr), :]
            return c

        lax.fori_loop(0, n_tiles, copy_tile, 0)
        cw = cw_ref[...]

        def tile(t, c):
            r0 = pl.multiple_of(t * tr, tr)
            win = xp[pl.ds(r0, tr + 16), :]
            acc = win[6:6 + tr] * cw[0:1]
            for kk in range(1, CONV_K):
                acc = acc + win[6 + kk:6 + kk + tr] * cw[kk:kk + 1]
            y = acc * _sigmoid(acc)
            if normalise:
                y = y * lax.rsqrt(jnp.sum(y * y, axis=-1, keepdims=True) + EPS)
            if scale != 1.0:
                y = y * scale
            dst[pl.ds(r0, tr), :] = y
            return c

        lax.fori_loop(0, n_tiles, tile, 0)

    conv_phase(q_ref, cwq_ref, qn, True, DN_HEAD_DIM ** -0.5)
    conv_phase(k_ref, cwk_ref, kn, True, 1.0)
    conv_phase(v_ref, cwv_ref, vn, False, 1.0)

    ri = lax.broadcasted_iota(I32, (PAIR, PAIR), 0)
    ci = lax.broadcasted_iota(I32, (PAIR, PAIR), 1)
    same = (ri // CHUNK) == (ci // CHUNK)
    mask_incl = (same & (ci <= ri), same & (ci >= ri))
    mask_strict = (same & (ci < ri), same & (ci > ri))
    eye = jnp.where(ri == ci, 1.0, 0.0).astype(F32)
    first_chunk = ci < CHUNK
    lu = jnp.concatenate([jnp.where(mask_incl[0], 1.0, 0.0), jnp.where(mask_incl[1], 1.0, 0.0)],
                         axis=0).astype(BF16)
    lu3 = jnp.concatenate([lu, lu, lu], axis=1)
    alog_row = alog_ref[...]
    dt_row = dt_ref[...]

    def lane_bcast(x, col):
        shifted = pltpu.roll(x, shift=lax.rem(LANES - col, LANES), axis=1)
        return jnp.broadcast_to(shifted[:, 0:1], x.shape)

    def wide(x):
        return jnp.concatenate([x, x], axis=1)

    same16 = wide((ri // 16) == (ci // 16))
    same32 = wide((ri // 32) == (ci // 32))
    off16 = same32 & jnp.logical_not(same16)
    eye_w = wide(eye)
    zero_blk = jnp.zeros((PAIR, PAIR), BF16)

    def bdiag(w):
        return jnp.concatenate([jnp.concatenate([w[:, :PAIR], zero_blk], axis=1),
                                jnp.concatenate([zero_blk, w[:, PAIR:]], axis=1)], axis=0)

    def tri_inverse_many(lws):
        d0s = [jnp.where(same16, lw, 0.0) for lw in lws]
        ms = [(-d0).astype(BF16) for d0 in d0s]
        xs = [eye_w - d0 for d0 in d0s]
        for _ in range(3):
            ms = [_dot(m, bdiag(m)).astype(BF16) for m in ms]
            xs = [x + _dot(x.astype(BF16), bdiag(m)) for x, m in zip(xs, ms)]
        for level in range(2):
            cs = [bdiag((jnp.where(off16, lw, 0.0) if level == 0 else jnp.where(same32, 0.0, lw)).astype(BF16))
                  for lw in lws]
            xbs = [x.astype(BF16) for x in xs]
            ts = [_dot(xb, c).astype(BF16) for xb, c in zip(xbs, cs)]
            xs = [x - _dot(t, bdiag(xb)) for x, t, xb in zip(xs, ts, xbs)]
        rs = []
        for lw, x in zip(lws, xs):
            lh, ll = _split2(lw)
            xh, xl = _split2(x)
            prod = _dot(jnp.concatenate([lh, lh, ll], axis=1),
                        jnp.concatenate([bdiag(xh), bdiag(xl), bdiag(xh)], axis=0))
            rs.append((eye_w - x) - prod)
        return [x + _dot(x.astype(BF16), bdiag(r.astype(BF16))) for x, r in zip(xs, rs)]

    def gate_stage(pairs):
        r0s = [pl.multiple_of(p * PAIR, PAIR) for p in pairs]
        blks = [ba_ref[0, pl.ds(r0, PAIR), :] for r0 in r0s]
        gs = [-jnp.exp(alog_row) * _softplus(blk + dt_row) for blk in blks]
        betas = [jnp.concatenate([lane_bcast(sg, head), lane_bcast(sg, DN_HEADS + head)], axis=1)
                 for sg in [_sigmoid(blk) for blk in blks]]
        g_reps = [jnp.concatenate([lane_bcast(g, 2 * DN_HEADS + head), lane_bcast(g, 3 * DN_HEADS + head)],
                                  axis=1) for g in gs]
        gps = [jnp.concatenate(_split3(g_rep), axis=0) for g_rep in g_reps]
        kqs = []
        for r0 in r0s:
            kb = kn[pl.ds(r0, PAIR), :].astype(BF16)
            kqs.append(_dot_nt(jnp.concatenate([kb, qn[pl.ds(r0, PAIR), :].astype(BF16)], axis=0), kb))
        css = [_dot(lu3, gp) for gp in gps]
        out = []
        for r0, beta, g_rep, cs, kq in zip(r0s, betas, g_reps, css, kqs):
            pre, suf = cs[:PAIR], cs[PAIR:]
            per_dir = ((beta[:, :LANES], pre[:, :LANES], suf[:, :LANES] - g_rep[:, :LANES]),
                       (beta[:, LANES:], suf[:, LANES:], pre[:, LANES:] - g_rep[:, LANES:]))
            out.append((r0, per_dir, kq[:PAIR], kq[PAIR:]))
        return out

    def finish_stage(chains, t_invs):
        egs, wus = [], []
        for (p, r0, d, beta, gc, ex, qk, dec), t_inv in zip(chains, t_invs):
            k2 = kn[pl.ds(r0, PAIR), :]
            eg = jnp.exp(gc)
            vb = (vn[pl.ds(r0, PAIR), :] * beta).astype(BF16)
            kbg = (k2 * beta * eg).astype(BF16)
            egs.append(eg)
            wus.append(_dot(t_inv.astype(BF16), jnp.concatenate([kbg, vb], axis=1)).astype(BF16))
        abs_, qos = [], []
        for (p, r0, d, beta, gc, ex, qk, dec), wu in zip(chains, wus):
            kst = (kn[pl.ds(r0, PAIR), :] * jnp.exp(ex)).T.astype(BF16)
            kst2 = jnp.concatenate([jnp.where(first_chunk, kst, 0.0), jnp.where(first_chunk, 0.0, kst)],
                                   axis=0).astype(BF16)
            abs_.append(_dot(kst2, wu))
            intra = jnp.where(mask_incl[d], qk * dec, 0.0).astype(BF16)
            qos.append(_dot(intra, wu))
        parts, comps = [], []
        for (p, r0, d, beta, gc, ex, qk, dec), eg, ab, qo in zip(chains, egs, abs_, qos):
            c1, c2 = (0, 1) if d == 0 else (1, 0)
            a = (-ab[:PAIR, :LANES], -ab[PAIR:, :LANES])
            b = (ab[:PAIR, LANES:], ab[PAIR:, LANES:])
            qp = qn[pl.ds(r0, PAIR), :] * eg - qo[:, :LANES]
            q = (qp[:CHUNK], qp[CHUNK:])
            comps.append(_dot(jnp.concatenate([a[c2], q[c2]], axis=0).astype(BF16),
                              jnp.concatenate([a[c1], b[c1]], axis=1).astype(BF16)))
            parts.append((a, b, q, c1, c2))
        for (p, r0, d, beta, gc, ex, qk, dec), qo, (a, b, q, c1, c2), comp in zip(chains, qos, parts, comps):
            etot = jnp.exp(gc + ex)
            e1, e2 = etot[c1 * CHUNK:c1 * CHUNK + 1], etot[c2 * CHUNK:c2 * CHUNK + 1]
            m = e2 * a[c1] + e1 * a[c2] + comp[:PAIR, :LANES]
            a0 = pl.multiple_of(p * AQ_ROWS, AQ_ROWS)
            aq_s[d, pl.ds(a0, PAIR), :] = m.astype(BF16)
            aq_s[d, pl.ds(a0 + PAIR, CHUNK), :] = q[c1].astype(BF16)
            aq_s[d, pl.ds(a0 + PAIR + CHUNK, CHUNK), :] = (e1 * q[c2] + comp[PAIR:, :LANES]).astype(BF16)
            b_s[d, pl.ds(r0, PAIR), :] = e2 * b[c1] + comp[:PAIR, LANES:] + b[c2]
            o1 = pl.ds(r0 + c1 * CHUNK, CHUNK)
            o2 = pl.ds(r0 + c2 * CHUNK, CHUNK)
            o_s[o1, :] = o_s[o1, :] + qo[c1 * CHUNK:(c1 + 1) * CHUNK, LANES:]
            o_s[o2, :] = o_s[o2, :] + qo[c2 * CHUNK:(c2 + 1) * CHUNK, LANES:] + comp[PAIR:, LANES:]
            e0 = pl.multiple_of(p * 8, 8)
            egl_s[d, pl.ds(e0, 8), :] = etot[0:8] * etot[CHUNK:CHUNK + 8]

    def prep_block(i, c):
        pairs = [i * ppb + j for j in range(ppb)]
        chains, lws = [], []
        for p, (r0, per_dir, kk, qk) in zip(pairs, gate_stage(pairs)):
            lms = []
            for d in range(2):
                beta, gc, ex = per_dir[d]
                m_in = mask_incl[d]
                dec = jnp.where(m_in, jnp.exp(jnp.where(m_in, gc - gc.T, 0.0)), 0.0)
                lms.append(jnp.where(mask_strict[d], beta * kk * dec, 0.0))
                chains.append((p, r0, d, beta, gc, ex, qk, dec))
            lws.append(jnp.concatenate(lms, axis=1))
        t_invs = []
        for t_w in tri_inverse_many(lws):
            t_invs += [t_w[:, :PAIR], t_w[:, PAIR:]]
        finish_stage(chains, t_invs)
        return c

    def zero_tile(t, c):
        r0 = pl.multiple_of(t * tr, tr)
        o_s[pl.ds(r0, tr), :] = jnp.zeros((tr, LANES), F32)
        return c

    lax.fori_loop(0, n_tiles, zero_tile, 0)
    lax.fori_loop(0, n_pairs // ppb, prep_block, 0)

    def pair_step(d, pair, state):
        c1 = d
        r0 = pl.multiple_of(pair * PAIR, PAIR)
        a0 = pl.multiple_of(pair * AQ_ROWS, AQ_ROWS)
        res = _dot(aq_s[d, pl.ds(a0, AQ_ROWS), :], state.astype(BF16))
        o1 = pl.ds(r0 + c1 * CHUNK, CHUNK)
        o2 = pl.ds(r0 + (1 - c1) * CHUNK, CHUNK)
        o_s[o1, :] = o_s[o1, :] + res[PAIR:PAIR + CHUNK]
        o_s[o2, :] = o_s[o2, :] + res[PAIR + CHUNK:]
        eg = egl_s[d, pl.ds(pl.multiple_of(pair * 8, 8), 8), :][0:1, :]
        return state * eg + res[:PAIR] + b_s[d, pl.ds(r0, PAIR), :]

    def pair_scan(n, carry):
        s_f, s_b = carry
        return pair_step(0, n, s_f), pair_step(1, n_pairs - 1 - n, s_b)

    zero_state = jnp.zeros((DN_HEAD_DIM, DN_HEAD_DIM), F32)
    lax.fori_loop(0, n_pairs, pair_scan, (zero_state, zero_state))

    og = og_ref[...]

    def out_tile(t, c):
        r0 = pl.multiple_of(t * tr, tr)
        o = o_s[pl.ds(r0, tr), :]
        z = z_ref[0, pl.ds(r0, tr), :]
        y = o * lax.rsqrt(jnp.mean(o * o, axis=-1, keepdims=True) + EPS) * og
        y_ref[0, pl.ds(r0, tr), :] = (y * (z * _sigmoid(z))).astype(BF16)
        return c

    lax.fori_loop(0, n_tiles, out_tile, 0)


def _dn_call(proj, conv_w, alog_row, dt_row, o_gain):
    b, s, _ = proj.shape
    hd = DN_HEAD_DIM
    tr = min(s, 256)
    qb, kb, vb, zb = COL_Q // hd, COL_K // hd, COL_V // hd, COL_Z // hd
    bab = COL_BA // LANES

    def col(base):
        return pl.BlockSpec((1, s, hd), lambda bi, h: (bi, 0, base + h))

    def cw(base):
        return pl.BlockSpec((CONV_K, hd), lambda bi, h: (0, base + h))

    row = pl.BlockSpec((1, LANES), lambda bi, h: (0, 0))
    n_pairs = s // PAIR
    ppb = next(n for n in (8, 4, 2, 1) if n_pairs % n == 0)
    kern = functools.partial(_dn_kernel, seq=s, tr=tr, ppb=ppb)
    return pl.pallas_call(
        kern,
        grid=(b, DN_HEADS),
        in_specs=[col(qb), col(kb), col(vb), col(zb),
                  pl.BlockSpec((1, s, LANES), lambda bi, h: (bi, 0, bab)),
                  cw(qb), cw(kb), cw(vb), row, row, row],
        out_specs=pl.BlockSpec((1, s, hd), lambda bi, h: (bi, 0, h)),
        out_shape=jax.ShapeDtypeStruct((b, s, DN_HEADS * hd), BF16),
        scratch_shapes=[pltpu.VMEM((s + 16, LANES), F32),
                        pltpu.VMEM((s, hd), F32), pltpu.VMEM((s, hd), F32), pltpu.VMEM((s, hd), F32),
                        pltpu.VMEM((2, s // PAIR * AQ_ROWS, hd), BF16),
                        pltpu.VMEM((2, s // PAIR * hd, hd), F32),
                        pltpu.VMEM((2, s // PAIR * 8, LANES), F32),
                        pltpu.VMEM((s, hd), F32)],
        compiler_params=_cparams(("parallel", "parallel")),
    )(proj, proj, proj, proj, proj, conv_w, conv_w, conv_w, alog_row, dt_row, o_gain.reshape(1, hd))


def _mlaproj_kernel(cq_ref, ckv_ref, kr_ref, pos_ref, invf_ref, sgn_ref, qg_ref, kvg_ref,
                    wqn_ref, wqr_ref, wqs_ref, wkn_ref, wv_ref,
                    q_ref, k_ref, v_ref):
    cq = cq_ref[0]
    hq = (cq * lax.rsqrt(jnp.mean(cq * cq, axis=-1, keepdims=True) + EPS) * qg_ref[...]).astype(BF16)
    qn = _dot(hq, wqn_ref[...])
    qr = _dot(hq, wqr_ref[...])
    qs = _dot(hq, wqs_ref[...])
    ang = pos_ref[0].astype(F32) * invf_ref[...]
    cc = jnp.cos(ang)
    ss = jnp.sin(ang) * sgn_ref[...]
    scale = QK_DIM ** -0.5 * math.log2(math.e)
    ckv = ckv_ref[0]
    hkv = (ckv * lax.rsqrt(jnp.mean(ckv * ckv, axis=-1, keepdims=True) + EPS) * kvg_ref[...]).astype(BF16)
    kn = _dot(hkv, wkn_ref[...])
    vv = _dot(hkv, wv_ref[...])
    kr = kr_ref[0]
    k_rope = (kr[:, :ROPE_DIM] * cc + kr[:, ROPE_DIM:] * ss).astype(BF16)
    for h in range(MLA_HEADS):
        lo, hi = h * ROPE_DIM, (h + 1) * ROPE_DIM
        q_rope = qr[:, lo:hi] * cc + qs[:, lo:hi] * ss
        q_ref[0, h, :, 0:NOPE_DIM] = (qn[:, h * NOPE_DIM:(h + 1) * NOPE_DIM] * scale).astype(BF16)
        q_ref[0, h, :, NOPE_DIM:QK_DIM] = (q_rope * scale).astype(BF16)
        k_ref[0, h, :, 0:NOPE_DIM] = kn[:, h * NOPE_DIM:(h + 1) * NOPE_DIM].astype(BF16)
        k_ref[0, h, :, NOPE_DIM:QK_DIM] = k_rope
        v_ref[0, h] = vv[:, h * V_DIM:(h + 1) * V_DIM].astype(BF16)


def _mlaproj_call(proj, pos3, invf, sgn, q_gain, kv_gain, wqn, wqr, wqs, wkn, wv):
    b, s, _ = proj.shape
    tm = min(s, 512)
    nh = MLA_HEADS

    def full(a):
        return pl.BlockSpec(a.shape, lambda bi, i: (0,) * a.ndim)

    qg = q_gain.reshape(1, Q_LORA)
    kvg = kv_gain.reshape(1, KV_LORA)
    return pl.pallas_call(
        _mlaproj_kernel,
        grid=(b, s // tm),
        in_specs=[pl.BlockSpec((1, tm, Q_LORA), lambda bi, i: (bi, i, COL_CQ // Q_LORA)),
                  pl.BlockSpec((1, tm, KV_LORA), lambda bi, i: (bi, i, COL_CKV // KV_LORA)),
                  pl.BlockSpec((1, tm, LANES), lambda bi, i: (bi, i, COL_KR // LANES)),
                  pl.BlockSpec((1, tm, 1), lambda bi, i: (bi, i, 0)),
                  full(invf), full(sgn), full(qg), full(kvg),
                  full(wqn), full(wqr), full(wqs), full(wkn), full(wv)],
        out_specs=[pl.BlockSpec((1, nh, tm, QK_DIM), lambda bi, i: (bi, 0, i, 0)),
                   pl.BlockSpec((1, nh, tm, QK_DIM), lambda bi, i: (bi, 0, i, 0)),
                   pl.BlockSpec((1, nh, tm, V_DIM), lambda bi, i: (bi, 0, i, 0))],
        out_shape=[jax.ShapeDtypeStruct((b, nh, s, QK_DIM), BF16),
                   jax.ShapeDtypeStruct((b, nh, s, QK_DIM), BF16),
                   jax.ShapeDtypeStruct((b, nh, s, V_DIM), BF16)],
        compiler_params=_cparams(("parallel", "parallel")),
    )(proj, proj, proj, pos3, invf, sgn, qg, kvg, wqn, wqr, wqs, wkn, wv)


def _attn_kernel(q_ref, k_ref, v_ref, o_ref, *, seq, tq, sub, tk):
    n_sub, n_c = tq // sub, seq // tk

    def scores(a, c):
        return _dot_nt(q_ref[0, 0, a * sub:(a + 1) * sub, :], k_ref[0, 0, c * tk:(c + 1) * tk, :])

    s_cur = [scores(0, c) for c in range(n_c)]
    for a in range(n_sub):
        mx = s_cur[0]
        for c in range(1, n_c):
            mx = jnp.maximum(mx, s_cur[c])
        m = jnp.max(mx, axis=-1, keepdims=True)
        lp = jnp.zeros((sub, LANES), F32)
        acc = jnp.zeros((sub, V_DIM), F32)
        s_next = []
        for c in range(n_c):
            p = jnp.exp2(s_cur[c] - m)
            for j in range(tk // LANES):
                lp = lp + p[:, j * LANES:(j + 1) * LANES]
            acc = acc + _dot(p.astype(BF16), v_ref[0, 0, c * tk:(c + 1) * tk, :])
            if a + 1 < n_sub:
                s_next.append(scores(a + 1, c))
        l = jnp.sum(lp, axis=-1, keepdims=True)
        o_ref[0, a * sub:(a + 1) * sub, :] = (acc / l).astype(BF16)
        s_cur = s_next


def _attn_call(q, k, v):
    b, nh, s, _ = q.shape
    tq = min(s, 2048)
    kern = functools.partial(_attn_kernel, seq=s, tq=tq, sub=min(tq, 256), tk=min(s, 512))
    return pl.pallas_call(
        kern,
        grid=(b, nh, s // tq),
        in_specs=[pl.BlockSpec((1, 1, tq, QK_DIM), lambda bi, h, i: (bi, h, i, 0)),
                  pl.BlockSpec((1, 1, s, QK_DIM), lambda bi, h, i: (bi, h, 0, 0)),
                  pl.BlockSpec((1, 1, s, V_DIM), lambda bi, h, i: (bi, h, 0, 0))],
        out_specs=pl.BlockSpec((1, tq, V_DIM), lambda bi, h, i: (bi, i, h)),
        out_shape=jax.ShapeDtypeStruct((b, s, nh * V_DIM), BF16),
        compiler_params=_cparams(("parallel", "parallel", "arbitrary")),
    )(q, k, v)


def _merge_kernel(ydn_ref, ymla_ref, gdn_ref, gmla_ref, x_ref, mod_ref, wodn_ref, womla_ref, wout_ref,
                  gffn_ref, wrt_ref,
                  x1_ref, h2_ref, aff_ref):
    y_dn = _dot(ydn_ref[0], wodn_ref[...])
    y_mla = _dot(ymla_ref[0], womla_ref[...])
    merged = _sigmoid(gdn_ref[0]) * y_dn + _sigmoid(gmla_ref[0]) * y_mla
    x1 = x_ref[0] + mod_ref[0, 2:3, :] * _dot(merged.astype(BF16), wout_ref[...])
    x1_ref[0] = x1
    y = x1 * lax.rsqrt(jnp.mean(x1 * x1, axis=-1, keepdims=True) + EPS) * gffn_ref[...]
    h2 = (y * (1.0 + mod_ref[0, 4:5, :]) + mod_ref[0, 3:4, :]).astype(BF16)
    h2_ref[0] = h2
    logits = _dot_nt(wrt_ref[...], h2)
    mx = jnp.max(logits, axis=0, keepdims=True)
    ex = jnp.exp(logits - mx)
    aff_ref[0] = ex / jnp.sum(ex, axis=0, keepdims=True)


def _merge_call(y_dn, y_mla, proj, x, mod3, w_o_dn, w_o_mla, w_out, g_ffn, w_router_t):
    b, s, d = x.shape
    tm = min(s, 512)
    ne = w_router_t.shape[0]

    def tok(width):
        return pl.BlockSpec((1, tm, width), lambda bi, i: (bi, i, 0))

    def full(a):
        return pl.BlockSpec(a.shape, lambda bi, i: (0,) * a.ndim)

    gf = g_ffn.reshape(1, d)
    return pl.pallas_call(
        _merge_kernel,
        grid=(b, s // tm),
        in_specs=[tok(d), tok(d),
                  pl.BlockSpec((1, tm, d), lambda bi, i: (bi, i, COL_GATE // d)),
                  pl.BlockSpec((1, tm, d), lambda bi, i: (bi, i, COL_GATE // d + 1)),
                  tok(d),
                  pl.BlockSpec((1, 6, d), lambda bi, i: (bi, 0, 0)),
                  full(w_o_dn), full(w_o_mla), full(w_out), full(gf), full(w_router_t)],
        out_specs=[tok(d), tok(d), pl.BlockSpec((1, ne, tm), lambda bi, i: (bi, 0, i))],
        out_shape=[jax.ShapeDtypeStruct((b, s, d), F32),
                   jax.ShapeDtypeStruct((b, s, d), BF16),
                   jax.ShapeDtypeStruct((b, ne, s), F32)],
        compiler_params=_cparams(("parallel", "parallel")),
    )(y_dn, y_mla, proj, proj, x, mod3, w_o_dn, w_o_mla, w_out, gf, w_router_t)


def _route_kernel(aff_ref, pos_ref, *, seq, cap):
    a = aff_ref[0]
    ne = a.shape[0]
    capf = jnp.float32(cap)

    def count_ge(t_bits):
        t = pltpu.bitcast(t_bits, F32)
        return jnp.sum(jnp.where(a >= t, 1.0, 0.0), axis=1, keepdims=True)

    def search(_, carry):
        lo, hi = carry
        mid = lo + ((hi - lo + 1) >> 1)
        ok = count_ge(mid) >= capf
        return jnp.where(ok, mid, lo), jnp.where(ok, hi, mid - 1)

    lo0 = jnp.zeros((ne, 1), I32)
    hi0 = jnp.full((ne, 1), 0x7F800000, I32)
    thr_bits, _ = lax.fori_loop(0, 32, search, (lo0, hi0))
    thr = pltpu.bitcast(thr_bits, F32)

    ur = lax.broadcasted_iota(I32, (LANES, LANES), 0)
    uc = lax.broadcasted_iota(I32, (LANES, LANES), 1)
    upper = jnp.where(ur <= uc, 1.0, 0.0).astype(BF16)

    def prefix_incl(mask):
        carry = jnp.zeros((ne, 1), F32)
        outs = []
        for j in range(seq // LANES):
            pr = _dot(mask[:, j * LANES:(j + 1) * LANES].astype(BF16), upper) + carry
            outs.append(pr)
            carry = pr[:, LANES - 1:LANES]
        return jnp.concatenate(outs, axis=1)

    gt = a > thr
    eq = a == thr
    n_gt = jnp.sum(jnp.where(gt, 1.0, 0.0), axis=1, keepdims=True)
    take_eq = eq & (prefix_incl(jnp.where(eq, 1.0, 0.0)) <= capf - n_gt)
    sel = gt | take_eq
    slot = prefix_incl(jnp.where(sel, 1.0, 0.0)) - 1.0
    pos_ref[0] = jnp.where(sel, slot, -1.0).astype(I32)


def _route_call(aff, cap):
    b, ne, s = aff.shape
    kern = functools.partial(_route_kernel, seq=s, cap=cap)
    return pl.pallas_call(
        kern,
        grid=(b,),
        in_specs=[pl.BlockSpec((1, ne, s), lambda bi: (bi, 0, 0))],
        out_specs=pl.BlockSpec((1, ne, s), lambda bi: (bi, 0, 0)),
        out_shape=jax.ShapeDtypeStruct((b, ne, s), I32),
        compiler_params=_cparams(("parallel",)),
    )(aff)


def _gather_kernel(pos_ref, aff_ref, h_ref, xe_ref, gate_ref, *, seq, cap, tk):
    slot = lax.broadcasted_iota(I32, (cap, 1), 0)
    acc = jnp.zeros((cap, h_ref.shape[-1]), F32)
    gacc = jnp.zeros((cap, 1), F32)
    for kt in range(seq // tk):
        pm = pos_ref[0, 0, :, kt * tk:(kt + 1) * tk]
        hit = pm == slot
        acc = acc + _dot(jnp.where(hit, 1.0, 0.0).astype(BF16), h_ref[0, kt * tk:(kt + 1) * tk, :])
        am = aff_ref[0, 0, :, kt * tk:(kt + 1) * tk]
        gacc = gacc + jnp.sum(jnp.where(hit, am, 0.0), axis=1, keepdims=True)
    xe_ref[0, 0] = acc.astype(BF16)
    gate_ref[0, 0] = jnp.broadcast_to(gacc, (cap, LANES))


def _gather_call(pos4, aff4, h2, cap):
    b, ne, _, s = pos4.shape
    d = h2.shape[-1]
    tk = min(s, 512)
    kern = functools.partial(_gather_kernel, seq=s, cap=cap, tk=tk)
    return pl.pallas_call(
        kern,
        grid=(b, ne),
        in_specs=[pl.BlockSpec((1, 1, 1, s), lambda bi, e: (bi, e, 0, 0)),
                  pl.BlockSpec((1, 1, 1, s), lambda bi, e: (bi, e, 0, 0)),
                  pl.BlockSpec((1, s, d), lambda bi, e: (bi, 0, 0))],
        out_specs=[pl.BlockSpec((1, 1, cap, d), lambda bi, e: (bi, e, 0, 0)),
                   pl.BlockSpec((1, 1, cap, LANES), lambda bi, e: (bi, e, 0, 0))],
        out_shape=[jax.ShapeDtypeStruct((b, ne, cap, d), BF16),
                   jax.ShapeDtypeStruct((b, ne, cap, LANES), F32)],
        compiler_params=_cparams(("parallel", "arbitrary")),
    )(pos4, aff4, h2)


def _ffn_kernel(xe_ref, gate_ref, wg_ref, wu_ref, wd_ref, ye_ref, wg_s, wu_s, wd_s):
    @pl.when(pl.program_id(1) == 0)
    def _():
        wg_s[...] = wg_ref[0].astype(BF16)
        wu_s[...] = wu_ref[0].astype(BF16)
        wd_s[...] = wd_ref[0].astype(BF16)

    x = xe_ref[0, 0]
    g = _dot(x, wg_s[...])
    u = _dot(x, wu_s[...])
    hid = (g * _sigmoid(g)) * u
    y = _dot(hid.astype(BF16), wd_s[...])
    ye_ref[0, 0] = (y * gate_ref[0, 0][:, 0:1]).astype(BF16)


def _ffn_call(xe, gate, wg, wu, wd):
    b, ne, cap, d = xe.shape
    f = wg.shape[-1]
    return pl.pallas_call(
        _ffn_kernel,
        grid=(ne, b),
        in_specs=[pl.BlockSpec((1, 1, cap, d), lambda e, bi: (bi, e, 0, 0)),
                  pl.BlockSpec((1, 1, cap, LANES), lambda e, bi: (bi, e, 0, 0)),
                  pl.BlockSpec((1, d, f), lambda e, bi: (e, 0, 0)),
                  pl.BlockSpec((1, d, f), lambda e, bi: (e, 0, 0)),
                  pl.BlockSpec((1, f, d), lambda e, bi: (e, 0, 0))],
        out_specs=pl.BlockSpec((1, 1, cap, d), lambda e, bi: (bi, e, 0, 0)),
        out_shape=jax.ShapeDtypeStruct((b, ne, cap, d), BF16),
        scratch_shapes=[pltpu.VMEM((d, f), BF16), pltpu.VMEM((d, f), BF16), pltpu.VMEM((f, d), BF16)],
        compiler_params=_cparams(("parallel", "arbitrary")),
    )(xe, gate, wg, wu, wd)


def _combine_kernel(pos_ref, ye_ref, x1_ref, mod_ref, gfin_ref, o_ref, acc_ref, *, cap, last_norm):
    g = pl.program_id(2)
    eg = ye_ref.shape[1]

    @pl.when(g == 0)
    def _():
        acc_ref[...] = jnp.zeros_like(acc_ref)

    slot = lax.broadcasted_iota(I32, (1, cap), 1)
    pos = pos_ref[0, 0]
    onehot = jnp.concatenate(
        [jnp.where(pos[:, j:j + 1] == slot, 1.0, 0.0).astype(BF16) for j in range(eg)], axis=1)
    acc_ref[...] += _dot(onehot, ye_ref[0].reshape(eg * cap, ye_ref.shape[-1]))

    @pl.when(g == pl.num_programs(2) - 1)
    def _():
        x2 = x1_ref[0] + mod_ref[0, 5:6, :] * acc_ref[...]
        if last_norm:
            x2 = x2 * lax.rsqrt(jnp.mean(x2 * x2, axis=-1, keepdims=True) + EPS) * gfin_ref[...]
        o_ref[0] = x2


def _combine_call(pos_tok, ye, x1, mod3, g_final, cap, last_norm):
    b, s, d = x1.shape
    ne = ye.shape[1]
    n_groups, eg = pos_tok.shape[1], pos_tok.shape[3]
    tm = min(s, 512)
    kern = functools.partial(_combine_kernel, cap=cap, last_norm=last_norm)
    return pl.pallas_call(
        kern,
        grid=(b, s // tm, n_groups),
        in_specs=[pl.BlockSpec((1, 1, tm, eg), lambda bi, i, e: (bi, e, i, 0)),
                  pl.BlockSpec((1, eg, cap, d), lambda bi, i, e: (bi, e, 0, 0)),
                  pl.BlockSpec((1, tm, d), lambda bi, i, e: (bi, i, 0)),
                  pl.BlockSpec((1, 6, d), lambda bi, i, e: (bi, 0, 0)),
                  pl.BlockSpec((1, d), lambda bi, i, e: (0, 0))],
        out_specs=pl.BlockSpec((1, tm, d), lambda bi, i, e: (bi, i, 0)),
        out_shape=jax.ShapeDtypeStruct((b, s, d), F32),
        scratch_shapes=[pltpu.VMEM((tm, d), F32)],
        compiler_params=_cparams(("parallel", "parallel", "arbitrary")),
    )(pos_tok, ye, x1, mod3, g_final.reshape(1, d))


def _pack_w_in(w):
    o_ba = 4 * 1024 + 0
    o_b = o_ba
    o_a = o_b + 2 * DN_HEADS
    o_cq = o_a + 2 * DN_HEADS
    o_ckv = o_cq + Q_LORA
    o_kr = o_ckv + KV_LORA
    o_g = o_kr + ROPE_DIM
    half = ROPE_DIM // 2
    w_kr = w[:, o_kr:o_g]
    w_kr_sw = jnp.concatenate([w_kr[:, half:], w_kr[:, :half]], axis=1)
    pad = jnp.zeros((w.shape[0], COL_GATE - COL_BA - 4 * DN_HEADS), w.dtype)
    packed = jnp.concatenate([w[:, :o_ba], w[:, o_cq:o_ckv], w[:, o_ckv:o_kr], w_kr, w_kr_sw,
                              w[:, o_b:o_a], w[:, o_a:o_cq], pad, w[:, o_g:]], axis=1)
    assert packed.shape[1] == N_PAD
    return packed.astype(BF16)


def _lane_row(vals, offset):
    row = jnp.zeros((1, LANES), F32)
    return row.at[0, offset:offset + vals.size].set(vals.reshape(-1).astype(F32))


def kernel(x, c, positions, w_mod, b_mod, g_mix, w_in, conv_w, a_log, dt_bias, dn_o_gain, q_gain, w_uq,
           kv_gain, w_ukv, w_o_dn, w_o_mla, w_out, g_ffn, w_router, w_gate, w_up, w_down, g_final):
    b, s, d = x.shape
    depth = w_mod.shape[0]
    cap = CAPACITY_FACTOR * s // N_EXPERTS
    half = ROPE_DIM // 2
    inv_freq = ROPE_THETA ** (-jnp.arange(half, dtype=F32) / half)
    invf = jnp.concatenate([inv_freq, inv_freq]).reshape(1, ROPE_DIM)
    sgn = jnp.concatenate([-jnp.ones((half,), F32), jnp.ones((half,), F32)]).reshape(1, ROPE_DIM)
    pos3 = positions.reshape(b, s, 1)
    c_pad = jnp.zeros((8, d), F32).at[:b].set(c)
    swap = np.concatenate([np.arange(half, ROPE_DIM), np.arange(half)])

    for l in range(depth):
        mod3 = _mod_call(c_pad, w_mod[l], b_mod[l])[:b].reshape(b, 6, d)
        proj = _inproj_call(x, mod3, g_mix[l], _pack_w_in(w_in[l]))

        alog_row = _lane_row(a_log[l], 2 * DN_HEADS)
        dt_row = _lane_row(dt_bias[l], 2 * DN_HEADS)
        y_dn = _dn_call(proj, conv_w[l], alog_row, dt_row, dn_o_gain[l])

        wq = w_uq[l].reshape(Q_LORA, MLA_HEADS, QK_DIM)
        wqn = wq[:, :, :NOPE_DIM].reshape(Q_LORA, MLA_HEADS * NOPE_DIM).astype(BF16)
        wqr = wq[:, :, NOPE_DIM:].reshape(Q_LORA, MLA_HEADS * ROPE_DIM).astype(BF16)
        wqs = wq[:, :, NOPE_DIM:][:, :, swap].reshape(Q_LORA, MLA_HEADS * ROPE_DIM).astype(BF16)
        wkv = w_ukv[l].reshape(KV_LORA, MLA_HEADS, NOPE_DIM + V_DIM)
        wkn = wkv[:, :, :NOPE_DIM].reshape(KV_LORA, MLA_HEADS * NOPE_DIM).astype(BF16)
        wv = wkv[:, :, NOPE_DIM:].reshape(KV_LORA, MLA_HEADS * V_DIM).astype(BF16)
        q, k, v = _mlaproj_call(proj, pos3, invf, sgn, q_gain[l], kv_gain[l], wqn, wqr, wqs, wkn, wv)
        y_mla = _attn_call(q, k, v)

        x1, h2, aff = _merge_call(y_dn, y_mla, proj, x, mod3, w_o_dn[l].astype(BF16),
                                  w_o_mla[l].astype(BF16), w_out[l].astype(BF16), g_ffn[l],
                                  w_router[l].T.astype(BF16))
        pos = _route_call(aff, cap)
        pos4 = pos.reshape(b, N_EXPERTS, 1, s)
        aff4 = aff.reshape(b, N_EXPERTS, 1, s)
        xe, gate = _gather_call(pos4, aff4, h2, cap)
        ye = _ffn_call(xe, gate, w_gate[l], w_up[l], w_down[l])
        pos_tok = pos.reshape(b, N_EXPERTS // COMBINE_GROUP, COMBINE_GROUP, s).transpose(0, 1, 3, 2)
        x = _combine_call(pos_tok, ye, x1, mod3, g_final, cap, last_norm=(l == depth - 1))
    return x
```

```python
import functools
import math

import jax
import jax.numpy as jnp
import numpy as np
from jax import lax
from jax.experimental import pallas as pl
from jax.experimental.pallas import tpu as pltpu

F32 = jnp.float32
BF16 = jnp.bfloat16
I32 = jnp.int32

EPS = 1e-6
DN_HEADS = 8
DN_HEAD_DIM = 128
CONV_K = 5
CHUNK = 64
PAIR = 2 * CHUNK
AQ_ROWS = DN_HEAD_DIM + PAIR
MLA_HEADS = 8
NOPE_DIM = 128
ROPE_DIM = 64
V_DIM = 128
QK_DIM = NOPE_DIM + ROPE_DIM
Q_LORA = 512
KV_LORA = 256
ROPE_THETA = 10000.0
N_EXPERTS = 16
CAPACITY_FACTOR = 2
COMBINE_GROUP = 4
LANES = 128
VMEM_LIMIT = 56 * 1024 * 1024

COL_Q, COL_K, COL_V, COL_Z = 0, 1024, 2048, 3072
COL_CQ, COL_CKV, COL_KR, COL_BA, COL_GATE = 4096, 4608, 4864, 4992, 5120
N_PAD = 7168


def _cparams(sem):
    return pltpu.CompilerParams(dimension_semantics=sem, vmem_limit_bytes=VMEM_LIMIT)


def _dot(a, b):
    return jnp.dot(a, b, preferred_element_type=F32)


def _dot_nt(a, b):
    return lax.dot_general(a, b, (((1,), (1,)), ((), ())), preferred_element_type=F32)


def _dot_tn(a, b):
    return lax.dot_general(a, b, (((0,), (0,)), ((), ())), preferred_element_type=F32)


def _sigmoid(x):
    return 1.0 / (1.0 + jnp.exp(-x))


def _softplus(x):
    return jnp.maximum(x, 0.0) + jnp.log(1.0 + jnp.exp(-jnp.abs(x)))


def _split2(x):
    h = x.astype(BF16)
    return h, (x - h.astype(F32)).astype(BF16)


def _split3(x):
    h = x.astype(BF16)
    r = x - h.astype(F32)
    m = r.astype(BF16)
    return h, m, (r - m.astype(F32)).astype(BF16)


def _dot3_parts(a_parts, b_parts):
    ah, al = a_parts
    bh, bl = b_parts
    return _dot(jnp.concatenate([ah, ah, al], axis=1), jnp.concatenate([bh, bl, bh], axis=0))


def _mod_kernel(c_ref, w_ref, b_ref, o_ref):
    c = c_ref[...]
    s = c * _sigmoid(c)
    o_ref[...] = _dot(s.astype(BF16), w_ref[...].astype(BF16)) + b_ref[...]


def _mod_call(c_pad, w_mod, b_mod):
    rows, d = c_pad.shape
    n = w_mod.shape[1]
    tn = 1024
    return pl.pallas_call(
        _mod_kernel,
        grid=(n // tn,),
        in_specs=[pl.BlockSpec((rows, d), lambda j: (0, 0)),
                  pl.BlockSpec((d, tn), lambda j: (0, j)),
                  pl.BlockSpec((1, tn), lambda j: (0, j))],
        out_specs=pl.BlockSpec((rows, tn), lambda j: (0, j)),
        out_shape=jax.ShapeDtypeStruct((rows, n), F32),
        compiler_params=_cparams(("arbitrary",)),
    )(c_pad, w_mod, b_mod.reshape(1, n))


def _inproj_kernel(x_ref, mod_ref, g_ref, w_ref, o_ref, ba_ref, h_scr, *, tn):
    j = pl.program_id(2)

    @pl.when(j == 0)
    def _():
        x = x_ref[0]
        ms = jnp.mean(x * x, axis=-1, keepdims=True)
        y = x * lax.rsqrt(ms + EPS) * g_ref[...]
        h = y * (1.0 + mod_ref[0, 1:2, :]) + mod_ref[0, 0:1, :]
        h_scr[...] = h.astype(BF16)

    acc = _dot(h_scr[...], w_ref[...])
    o_ref[0] = acc.astype(BF16)

    @pl.when(j == COL_BA // tn)
    def _():
        ba_ref[0] = acc[:, COL_BA % tn:COL_BA % tn + LANES]


def _inproj_call(x, mod3, g_mix, w_in_p):
    b, s, d = x.shape
    n = w_in_p.shape[1]
    tm = min(s, 1024)
    tn = 1024
    return pl.pallas_call(
        functools.partial(_inproj_kernel, tn=tn),
        grid=(b, s // tm, n // tn),
        in_specs=[pl.BlockSpec((1, tm, d), lambda bi, i, j: (bi, i, 0)),
                  pl.BlockSpec((1, 6, d), lambda bi, i, j: (bi, 0, 0)),
                  pl.BlockSpec((1, d), lambda bi, i, j: (0, 0)),
                  pl.BlockSpec((d, tn), lambda bi, i, j: (0, j))],
        out_specs=[pl.BlockSpec((1, tm, tn), lambda bi, i, j: (bi, i, j)),
                   pl.BlockSpec((1, tm, LANES), lambda bi, i, j: (bi, i, 0))],
        out_shape=[jax.ShapeDtypeStruct((b, s, n), BF16),
                   jax.ShapeDtypeStruct((b, s, LANES), F32)],
        scratch_shapes=[pltpu.VMEM((tm, d), BF16)],
        compiler_params=_cparams(("parallel", "parallel", "arbitrary")),
    )(x, mod3, g_mix.reshape(1, d), w_in_p)


def _dn_kernel(q_ref, k_ref, v_ref, z_ref, ba_ref, cwq_ref, cwk_ref, cwv_ref, alog_ref, dt_ref, og_ref,
               y_ref,
               xp, qn, kn, vn, aq_s, b_s, egl_s, o_s, *, seq, tr, ppb):
    head = pl.program_id(1)
    n_pairs = seq // PAIR
    n_tiles = seq // tr

    def conv_phase(x_ref, cw_ref, dst, normalise, scale):
        xp[0:8, :] = jnp.zeros((8, LANES), F32)
        xp[seq + 8:seq + 16, :] = jnp.zeros((8, LANES), F32)

        def copy_tile(t, c):
            r0 = pl.multiple_of(t * tr, tr)
            xp[pl.ds(r0 + 8, tr), :] = x_ref[0, pl.ds(r0, tr), :].astype(F32)
            return c

        lax.fori_loop(0, n_tiles, copy_tile, 0)
        cw = cw_ref[...]

        def tile(t, c):
            r0 = pl.multiple_of(t * tr, tr)
            acc = xp[pl.ds(r0 + 6, tr), :] * cw[0:1]
            for kk in range(1, CONV_K):
                acc = acc + xp[pl.ds(r0 + 6 + kk, tr), :] * cw[kk:kk + 1]
            y = acc * _sigmoid(acc)
            if normalise:
                y = y * lax.rsqrt(jnp.sum(y * y, axis=-1, keepdims=True) + EPS)
            if scale != 1.0:
                y = y * scale
            dst[pl.ds(r0, tr), :] = y
            return c

        lax.fori_loop(0, n_tiles, tile, 0)

    conv_phase(q_ref, cwq_ref, qn, True, DN_HEAD_DIM ** -0.5)
    conv_phase(k_ref, cwk_ref, kn, True, 1.0)
    conv_phase(v_ref, cwv_ref, vn, False, 1.0)

    ri = lax.broadcasted_iota(I32, (PAIR, PAIR), 0)
    ci = lax.broadcasted_iota(I32, (PAIR, PAIR), 1)
    same = (ri // CHUNK) == (ci // CHUNK)
    mask_incl = (same & (ci <= ri), same & (ci >= ri))
    mask_strict = (same & (ci < ri), same & (ci > ri))
    eye = jnp.where(ri == ci, 1.0, 0.0).astype(F32)
    first_chunk = ci < CHUNK
    lu = jnp.concatenate([jnp.where(mask_incl[0], 1.0, 0.0), jnp.where(mask_incl[1], 1.0, 0.0)],
                         axis=0).astype(BF16)
    lu3 = jnp.concatenate([lu, lu, lu], axis=1)
    alog_row = alog_ref[...]
    dt_row = dt_ref[...]

    def lane_bcast(x, col):
        shifted = pltpu.roll(x, shift=lax.rem(LANES - col, LANES), axis=1)
        return jnp.broadcast_to(shifted[:, 0:1], x.shape)

    def wide(x):
        return jnp.concatenate([x, x], axis=1)

    same16 = wide((ri // 16) == (ci // 16))
    same32 = wide((ri // 32) == (ci // 32))
    off16 = same32 & jnp.logical_not(same16)
    eye_w = wide(eye)
    zero_blk = jnp.zeros((PAIR, PAIR), BF16)

    def bdiag(w):
        return jnp.concatenate([jnp.concatenate([w[:, :PAIR], zero_blk], axis=1),
                                jnp.concatenate([zero_blk, w[:, PAIR:]], axis=1)], axis=0)

    def tri_inverse_many(lws):
        d0s = [jnp.where(same16, lw, 0.0) for lw in lws]
        ms = [(-d0).astype(BF16) for d0 in d0s]
        xs = [eye_w - d0 for d0 in d0s]
        for _ in range(3):
            ms = [_dot(m, bdiag(m)).astype(BF16) for m in ms]
            xs = [x + _dot(x.astype(BF16), bdiag(m)) for x, m in zip(xs, ms)]
        for level in range(2):
            cs = [bdiag((jnp.where(off16, lw, 0.0) if level == 0 else jnp.where(same32, 0.0, lw)).astype(BF16))
                  for lw in lws]
            xbs = [x.astype(BF16) for x in xs]
            ts = [_dot(xb, c).astype(BF16) for xb, c in zip(xbs, cs)]
            xs = [x - _dot(t, bdiag(xb)) for x, t, xb in zip(xs, ts, xbs)]
        rs = []
        for lw, x in zip(lws, xs):
            lh, ll = _split2(lw)
            xh, xl = _split2(x)
            prod = _dot(jnp.concatenate([lh, lh, ll], axis=1),
                        jnp.concatenate([bdiag(xh), bdiag(xl), bdiag(xh)], axis=0))
            rs.append((eye_w - x) - prod)
        return [x + _dot(x.astype(BF16), bdiag(r.astype(BF16))) for x, r in zip(xs, rs)]

    def gate_stage(pairs):
        r0s = [pl.multiple_of(p * PAIR, PAIR) for p in pairs]
        blks = [ba_ref[0, pl.ds(r0, PAIR), :] for r0 in r0s]
        gs = [-jnp.exp(alog_row) * _softplus(blk + dt_row) for blk in blks]
        betas = [jnp.concatenate([lane_bcast(sg, head), lane_bcast(sg, DN_HEADS + head)], axis=1)
                 for sg in [_sigmoid(blk) for blk in blks]]
        g_reps = [jnp.concatenate([lane_bcast(g, 2 * DN_HEADS + head), lane_bcast(g, 3 * DN_HEADS + head)],
                                  axis=1) for g in gs]
        gps = [jnp.concatenate(_split3(g_rep), axis=0) for g_rep in g_reps]
        kqs = []
        for r0 in r0s:
            kb = kn[pl.ds(r0, PAIR), :].astype(BF16)
            kqs.append(_dot_nt(jnp.concatenate([kb, qn[pl.ds(r0, PAIR), :].astype(BF16)], axis=0), kb))
        css = [_dot(lu3, gp) for gp in gps]
        out = []
        for r0, beta, g_rep, cs, kq in zip(r0s, betas, g_reps, css, kqs):
            pre, suf = cs[:PAIR], cs[PAIR:]
            per_dir = ((beta[:, :LANES], pre[:, :LANES], suf[:, :LANES] - g_rep[:, :LANES]),
                       (beta[:, LANES:], suf[:, LANES:], pre[:, LANES:] - g_rep[:, LANES:]))
            out.append((r0, per_dir, kq[:PAIR], kq[PAIR:]))
        return out

    def finish_stage(chains, t_invs):
        egs, wus = [], []
        for (p, r0, d, beta, gc, ex, qk, dec), t_inv in zip(chains, t_invs):
            k2 = kn[pl.ds(r0, PAIR), :]
            eg = jnp.exp(gc)
            vb = (vn[pl.ds(r0, PAIR), :] * beta).astype(BF16)
            kbg = (k2 * beta * eg).astype(BF16)
            egs.append(eg)
            wus.append(_dot(t_inv.astype(BF16), jnp.concatenate([kbg, vb], axis=1)).astype(BF16))
        abs_, qos = [], []
        for (p, r0, d, beta, gc, ex, qk, dec), wu in zip(chains, wus):
            kst = (kn[pl.ds(r0, PAIR), :] * jnp.exp(ex)).T.astype(BF16)
            kst2 = jnp.concatenate([jnp.where(first_chunk, kst, 0.0), jnp.where(first_chunk, 0.0, kst)],
                                   axis=0).astype(BF16)
            abs_.append(_dot(kst2, wu))
            intra = jnp.where(mask_incl[d], qk * dec, 0.0).astype(BF16)
            qos.append(_dot(intra, wu))
        parts, comps = [], []
        for (p, r0, d, beta, gc, ex, qk, dec), eg, ab, qo in zip(chains, egs, abs_, qos):
            c1, c2 = (0, 1) if d == 0 else (1, 0)
            a = (-ab[:PAIR, :LANES], -ab[PAIR:, :LANES])
            b = (ab[:PAIR, LANES:], ab[PAIR:, LANES:])
            qp = qn[pl.ds(r0, PAIR), :] * eg - qo[:, :LANES]
            q = (qp[:CHUNK], qp[CHUNK:])
            comps.append(_dot(jnp.concatenate([a[c2], q[c2]], axis=0).astype(BF16),
                              jnp.concatenate([a[c1], b[c1]], axis=1).astype(BF16)))
            parts.append((a, b, q, c1, c2))
        for (p, r0, d, beta, gc, ex, qk, dec), qo, (a, b, q, c1, c2), comp in zip(chains, qos, parts, comps):
            etot = jnp.exp(gc + ex)
            e1, e2 = etot[c1 * CHUNK:c1 * CHUNK + 1], etot[c2 * CHUNK:c2 * CHUNK + 1]
            m = e2 * a[c1] + e1 * a[c2] + comp[:PAIR, :LANES]
            a0 = pl.multiple_of(p * AQ_ROWS, AQ_ROWS)
            aq_s[d, pl.ds(a0, PAIR), :] = m.astype(BF16)
            aq_s[d, pl.ds(a0 + PAIR, CHUNK), :] = q[c1].astype(BF16)
            aq_s[d, pl.ds(a0 + PAIR + CHUNK, CHUNK), :] = (e1 * q[c2] + comp[PAIR:, :LANES]).astype(BF16)
            b_s[d, pl.ds(r0, PAIR), :] = e2 * b[c1] + comp[:PAIR, LANES:] + b[c2]
            o1 = pl.ds(r0 + c1 * CHUNK, CHUNK)
            o2 = pl.ds(r0 + c2 * CHUNK, CHUNK)
            o_s[o1, :] = o_s[o1, :] + qo[c1 * CHUNK:(c1 + 1) * CHUNK, LANES:]
            o_s[o2, :] = o_s[o2, :] + qo[c2 * CHUNK:(c2 + 1) * CHUNK, LANES:] + comp[PAIR:, LANES:]
            e0 = pl.multiple_of(p * 8, 8)
            egl_s[d, pl.ds(e0, 8), :] = etot[0:8] * etot[CHUNK:CHUNK + 8]

    def prep_block(i, c):
        pairs = [i * ppb + j for j in range(ppb)]
        chains, lws = [], []
        for p, (r0, per_dir, kk, qk) in zip(pairs, gate_stage(pairs)):
            lms = []
            for d in range(2):
                beta, gc, ex = per_dir[d]
                m_in = mask_incl[d]
                dec = jnp.where(m_in, jnp.exp(jnp.where(m_in, gc - gc.T, 0.0)), 0.0)
                lms.append(jnp.where(mask_strict[d], beta * kk * dec, 0.0))
                chains.append((p, r0, d, beta, gc, ex, qk, dec))
            lws.append(jnp.concatenate(lms, axis=1))
        t_invs = []
        for t_w in tri_inverse_many(lws):
            t_invs += [t_w[:, :PAIR], t_w[:, PAIR:]]
        finish_stage(chains, t_invs)
        return c

    def zero_tile(t, c):
        r0 = pl.multiple_of(t * tr, tr)
        o_s[pl.ds(r0, tr), :] = jnp.zeros((tr, LANES), F32)
        return c

    lax.fori_loop(0, n_tiles, zero_tile, 0)
    lax.fori_loop(0, n_pairs // ppb, prep_block, 0)

    def pair_step(d, pair, state):
        c1 = d
        r0 = pl.multiple_of(pair * PAIR, PAIR)
        a0 = pl.multiple_of(pair * AQ_ROWS, AQ_ROWS)
        res = _dot(aq_s[d, pl.ds(a0, AQ_ROWS), :], state.astype(BF16))
        o1 = pl.ds(r0 + c1 * CHUNK, CHUNK)
        o2 = pl.ds(r0 + (1 - c1) * CHUNK, CHUNK)
        o_s[o1, :] = o_s[o1, :] + res[PAIR:PAIR + CHUNK]
        o_s[o2, :] = o_s[o2, :] + res[PAIR + CHUNK:]
        eg = egl_s[d, pl.ds(pl.multiple_of(pair * 8, 8), 8), :][0:1, :]
        return state * eg + res[:PAIR] + b_s[d, pl.ds(r0, PAIR), :]

    def pair_scan(n, carry):
        s_f, s_b = carry
        return pair_step(0, n, s_f), pair_step(1, n_pairs - 1 - n, s_b)

    zero_state = jnp.zeros((DN_HEAD_DIM, DN_HEAD_DIM), F32)
    lax.fori_loop(0, n_pairs, pair_scan, (zero_state, zero_state))

    og = og_ref[...]

    def out_tile(t, c):
        r0 = pl.multiple_of(t * tr, tr)
        o = o_s[pl.ds(r0, tr), :]
        z = z_ref[0, pl.ds(r0, tr), :].astype(F32)
        y = o * lax.rsqrt(jnp.mean(o * o, axis=-1, keepdims=True) + EPS) * og
        y_ref[0, pl.ds(r0, tr), :] = (y * (z * _sigmoid(z))).astype(BF16)
        return c

    lax.fori_loop(0, n_tiles, out_tile, 0)


def _dn_call(proj, ba, conv_w, alog_row, dt_row, o_gain):
    b, s, _ = proj.shape
    hd = DN_HEAD_DIM
    tr = min(s, 256)
    qb, kb, vb, zb = COL_Q // hd, COL_K // hd, COL_V // hd, COL_Z // hd

    def col(base):
        return pl.BlockSpec((1, s, hd), lambda bi, h: (bi, 0, base + h))

    def cw(base):
        return pl.BlockSpec((CONV_K, hd), lambda bi, h: (0, base + h))

    row = pl.BlockSpec((1, LANES), lambda bi, h: (0, 0))
    n_pairs = s // PAIR
    ppb = next(n for n in (8, 4, 2, 1) if n_pairs % n == 0)
    kern = functools.partial(_dn_kernel, seq=s, tr=tr, ppb=ppb)
    return pl.pallas_call(
        kern,
        grid=(b, DN_HEADS),
        in_specs=[col(qb), col(kb), col(vb), col(zb),
                  pl.BlockSpec((1, s, LANES), lambda bi, h: (bi, 0, 0)),
                  cw(qb), cw(kb), cw(vb), row, row, row],
        out_specs=pl.BlockSpec((1, s, hd), lambda bi, h: (bi, 0, h)),
        out_shape=jax.ShapeDtypeStruct((b, s, DN_HEADS * hd), BF16),
        scratch_shapes=[pltpu.VMEM((s + 16, LANES), F32),
                        pltpu.VMEM((s, hd), F32), pltpu.VMEM((s, hd), F32), pltpu.VMEM((s, hd), F32),
                        pltpu.VMEM((2, s // PAIR * AQ_ROWS, hd), BF16),
                        pltpu.VMEM((2, s // PAIR * hd, hd), F32),
                        pltpu.VMEM((2, s // PAIR * 8, LANES), F32),
                        pltpu.VMEM((s, hd), F32)],
        compiler_params=_cparams(("parallel", "parallel")),
    )(proj, proj, proj, proj, ba, conv_w, conv_w, conv_w, alog_row, dt_row, o_gain.reshape(1, hd))


def _mlaproj_kernel(cq_ref, ckv_ref, kr_ref, pos_ref, invf_ref, sgn_ref, qg_ref, kvg_ref,
                    wqn_ref, wqr_ref, wqs_ref, wkn_ref, wv_ref,
                    q_ref, k_ref, v_ref):
    cq = cq_ref[0].astype(F32)
    hq = (cq * lax.rsqrt(jnp.mean(cq * cq, axis=-1, keepdims=True) + EPS) * qg_ref[...]).astype(BF16)
    qn = _dot(hq, wqn_ref[...])
    qr = _dot(hq, wqr_ref[...])
    qs = _dot(hq, wqs_ref[...])
    ang = pos_ref[0].astype(F32) * invf_ref[...]
    cc = jnp.cos(ang)
    ss = jnp.sin(ang) * sgn_ref[...]
    scale = QK_DIM ** -0.5 * math.log2(math.e)
    ckv = ckv_ref[0].astype(F32)
    hkv = (ckv * lax.rsqrt(jnp.mean(ckv * ckv, axis=-1, keepdims=True) + EPS) * kvg_ref[...]).astype(BF16)
    kn = _dot(hkv, wkn_ref[...])
    vv = _dot(hkv, wv_ref[...])
    kr = kr_ref[0].astype(F32)
    k_rope = (kr[:, :ROPE_DIM] * cc + kr[:, ROPE_DIM:] * ss).astype(BF16)
    for h in range(MLA_HEADS):
        lo, hi = h * ROPE_DIM, (h + 1) * ROPE_DIM
        q_rope = qr[:, lo:hi] * cc + qs[:, lo:hi] * ss
        q_ref[0, h, :, 0:NOPE_DIM] = (qn[:, h * NOPE_DIM:(h + 1) * NOPE_DIM] * scale).astype(BF16)
        q_ref[0, h, :, NOPE_DIM:QK_DIM] = (q_rope * scale).astype(BF16)
        k_ref[0, h, :, 0:NOPE_DIM] = kn[:, h * NOPE_DIM:(h + 1) * NOPE_DIM].astype(BF16)
        k_ref[0, h, :, NOPE_DIM:QK_DIM] = k_rope
        v_ref[0, h] = vv[:, h * V_DIM:(h + 1) * V_DIM].astype(BF16)


def _mlaproj_call(proj, pos3, invf, sgn, q_gain, kv_gain, wqn, wqr, wqs, wkn, wv):
    b, s, _ = proj.shape
    tm = min(s, 512)
    nh = MLA_HEADS

    def full(a):
        return pl.BlockSpec(a.shape, lambda bi, i: (0,) * a.ndim)

    qg = q_gain.reshape(1, Q_LORA)
    kvg = kv_gain.reshape(1, KV_LORA)
    return pl.pallas_call(
        _mlaproj_kernel,
        grid=(b, s // tm),
        in_specs=[pl.BlockSpec((1, tm, Q_LORA), lambda bi, i: (bi, i, COL_CQ // Q_LORA)),
                  pl.BlockSpec((1, tm, KV_LORA), lambda bi, i: (bi, i, COL_CKV // KV_LORA)),
                  pl.BlockSpec((1, tm, LANES), lambda bi, i: (bi, i, COL_KR // LANES)),
                  pl.BlockSpec((1, tm, 1), lambda bi, i: (bi, i, 0)),
                  full(invf), full(sgn), full(qg), full(kvg),
                  full(wqn), full(wqr), full(wqs), full(wkn), full(wv)],
        out_specs=[pl.BlockSpec((1, nh, tm, QK_DIM), lambda bi, i: (bi, 0, i, 0)),
                   pl.BlockSpec((1, nh, tm, QK_DIM), lambda bi, i: (bi, 0, i, 0)),
                   pl.BlockSpec((1, nh, tm, V_DIM), lambda bi, i: (bi, 0, i, 0))],
        out_shape=[jax.ShapeDtypeStruct((b, nh, s, QK_DIM), BF16),
                   jax.ShapeDtypeStruct((b, nh, s, QK_DIM), BF16),
                   jax.ShapeDtypeStruct((b, nh, s, V_DIM), BF16)],
        compiler_params=_cparams(("parallel", "parallel")),
    )(proj, proj, proj, pos3, invf, sgn, qg, kvg, wqn, wqr, wqs, wkn, wv)


def _attn_kernel(q_ref, k_ref, v_ref, o_ref, *, seq, tq, sub, tk):
    n_sub, n_c = tq // sub, seq // tk

    def scores(a, c):
        return _dot_nt(q_ref[0, 0, a * sub:(a + 1) * sub, :], k_ref[0, 0, c * tk:(c + 1) * tk, :])

    s_cur = [scores(0, c) for c in range(n_c)]
    for a in range(n_sub):
        mx = s_cur[0]
        for c in range(1, n_c):
            mx = jnp.maximum(mx, s_cur[c])
        m = jnp.max(mx, axis=-1, keepdims=True)
        lp = jnp.zeros((sub, LANES), F32)
        acc = jnp.zeros((sub, V_DIM), F32)
        s_next = []
        for c in range(n_c):
            p = jnp.exp2(s_cur[c] - m)
            for j in range(tk // LANES):
                lp = lp + p[:, j * LANES:(j + 1) * LANES]
            acc = acc + _dot(p.astype(BF16), v_ref[0, 0, c * tk:(c + 1) * tk, :])
            if a + 1 < n_sub:
                s_next.append(scores(a + 1, c))
        l = jnp.sum(lp, axis=-1, keepdims=True)
        o_ref[0, a * sub:(a + 1) * sub, :] = (acc / l).astype(BF16)
        s_cur = s_next


def _attn_call(q, k, v):
    b, nh, s, _ = q.shape
    tq = min(s, 2048)
    kern = functools.partial(_attn_kernel, seq=s, tq=tq, sub=min(tq, 256), tk=min(s, 512))
    return pl.pallas_call(
        kern,
        grid=(b, nh, s // tq),
        in_specs=[pl.BlockSpec((1, 1, tq, QK_DIM), lambda bi, h, i: (bi, h, i, 0)),
                  pl.BlockSpec((1, 1, s, QK_DIM), lambda bi, h, i: (bi, h, 0, 0)),
                  pl.BlockSpec((1, 1, s, V_DIM), lambda bi, h, i: (bi, h, 0, 0))],
        out_specs=pl.BlockSpec((1, tq, V_DIM), lambda bi, h, i: (bi, i, h)),
        out_shape=jax.ShapeDtypeStruct((b, s, nh * V_DIM), BF16),
        compiler_params=_cparams(("parallel", "parallel", "arbitrary")),
    )(q, k, v)


def _merge_kernel(ydn_ref, ymla_ref, gdn_ref, gmla_ref, x_ref, mod_ref, wodn_ref, womla_ref, wout_ref,
                  gffn_ref, wrt_ref,
                  x1_ref, h2_ref, aff_ref):
    y_dn = _dot(ydn_ref[0], wodn_ref[...])
    y_mla = _dot(ymla_ref[0], womla_ref[...])
    merged = _sigmoid(gdn_ref[0].astype(F32)) * y_dn + _sigmoid(gmla_ref[0].astype(F32)) * y_mla
    x1 = x_ref[0] + mod_ref[0, 2:3, :] * _dot(merged.astype(BF16), wout_ref[...])
    x1_ref[0] = x1
    y = x1 * lax.rsqrt(jnp.mean(x1 * x1, axis=-1, keepdims=True) + EPS) * gffn_ref[...]
    h2 = (y * (1.0 + mod_ref[0, 4:5, :]) + mod_ref[0, 3:4, :]).astype(BF16)
    h2_ref[0] = h2
    logits = _dot_nt(wrt_ref[...], h2)
    mx = jnp.max(logits, axis=0, keepdims=True)
    ex = jnp.exp(logits - mx)
    aff_ref[0] = ex / jnp.sum(ex, axis=0, keepdims=True)


def _merge_call(y_dn, y_mla, proj, x, mod3, w_o_dn, w_o_mla, w_out, g_ffn, w_router_t):
    b, s, d = x.shape
    tm = min(s, 512)
    ne = w_router_t.shape[0]

    def tok(width):
        return pl.BlockSpec((1, tm, width), lambda bi, i: (bi, i, 0))

    def full(a):
        return pl.BlockSpec(a.shape, lambda bi, i: (0,) * a.ndim)

    gf = g_ffn.reshape(1, d)
    return pl.pallas_call(
        _merge_kernel,
        grid=(b, s // tm),
        in_specs=[tok(d), tok(d),
                  pl.BlockSpec((1, tm, d), lambda bi, i: (bi, i, COL_GATE // d)),
                  pl.BlockSpec((1, tm, d), lambda bi, i: (bi, i, COL_GATE // d + 1)),
                  tok(d),
                  pl.BlockSpec((1, 6, d), lambda bi, i: (bi, 0, 0)),
                  full(w_o_dn), full(w_o_mla), full(w_out), full(gf), full(w_router_t)],
        out_specs=[tok(d), tok(d), pl.BlockSpec((1, ne, tm), lambda bi, i: (bi, 0, i))],
        out_shape=[jax.ShapeDtypeStruct((b, s, d), F32),
                   jax.ShapeDtypeStruct((b, s, d), BF16),
                   jax.ShapeDtypeStruct((b, ne, s), F32)],
        compiler_params=_cparams(("parallel", "parallel")),
    )(y_dn, y_mla, proj, proj, x, mod3, w_o_dn, w_o_mla, w_out, gf, w_router_t)


def _route_kernel(aff_ref, pos_ref, *, seq, cap):
    a = aff_ref[0]
    ne = a.shape[0]
    capf = jnp.float32(cap)

    def count_ge(t_bits):
        t = pltpu.bitcast(t_bits, F32)
        return jnp.sum(jnp.where(a >= t, 1.0, 0.0), axis=1, keepdims=True)

    def search(_, carry):
        lo, hi = carry
        mid = lo + ((hi - lo + 1) >> 1)
        ok = count_ge(mid) >= capf
        return jnp.where(ok, mid, lo), jnp.where(ok, hi, mid - 1)

    lo0 = jnp.zeros((ne, 1), I32)
    hi0 = jnp.full((ne, 1), 0x7F800000, I32)
    thr_bits, _ = lax.fori_loop(0, 32, search, (lo0, hi0))
    thr = pltpu.bitcast(thr_bits, F32)

    ur = lax.broadcasted_iota(I32, (LANES, LANES), 0)
    uc = lax.broadcasted_iota(I32, (LANES, LANES), 1)
    upper = jnp.where(ur <= uc, 1.0, 0.0).astype(BF16)

    def prefix_incl(mask):
        carry = jnp.zeros((ne, 1), F32)
        outs = []
        for j in range(seq // LANES):
            pr = _dot(mask[:, j * LANES:(j + 1) * LANES].astype(BF16), upper) + carry
            outs.append(pr)
            carry = pr[:, LANES - 1:LANES]
        return jnp.concatenate(outs, axis=1)

    gt = a > thr
    eq = a == thr
    n_gt = jnp.sum(jnp.where(gt, 1.0, 0.0), axis=1, keepdims=True)
    take_eq = eq & (prefix_incl(jnp.where(eq, 1.0, 0.0)) <= capf - n_gt)
    sel = gt | take_eq
    slot = prefix_incl(jnp.where(sel, 1.0, 0.0)) - 1.0
    pos_ref[0] = jnp.where(sel, slot, -1.0).astype(I32)


def _route_call(aff, cap):
    b, ne, s = aff.shape
    kern = functools.partial(_route_kernel, seq=s, cap=cap)
    return pl.pallas_call(
        kern,
        grid=(b,),
        in_specs=[pl.BlockSpec((1, ne, s), lambda bi: (bi, 0, 0))],
        out_specs=pl.BlockSpec((1, ne, s), lambda bi: (bi, 0, 0)),
        out_shape=jax.ShapeDtypeStruct((b, ne, s), I32),
        compiler_params=_cparams(("parallel",)),
    )(aff)


def _gather_kernel(pos_ref, aff_ref, h_ref, xe_ref, gate_ref, *, seq, cap, tk):
    slot = lax.broadcasted_iota(I32, (cap, 1), 0)
    acc = jnp.zeros((cap, h_ref.shape[-1]), F32)
    gacc = jnp.zeros((cap, 1), F32)
    for kt in range(seq // tk):
        pm = pos_ref[0, 0, :, kt * tk:(kt + 1) * tk]
        hit = pm == slot
        acc = acc + _dot(jnp.where(hit, 1.0, 0.0).astype(BF16), h_ref[0, kt * tk:(kt + 1) * tk, :])
        am = aff_ref[0, 0, :, kt * tk:(kt + 1) * tk]
        gacc = gacc + jnp.sum(jnp.where(hit, am, 0.0), axis=1, keepdims=True)
    xe_ref[0, 0] = acc.astype(BF16)
    gate_ref[0, 0] = jnp.broadcast_to(gacc, (cap, LANES))


def _gather_call(pos4, aff4, h2, cap):
    b, ne, _, s = pos4.shape
    d = h2.shape[-1]
    tk = min(s, 512)
    kern = functools.partial(_gather_kernel, seq=s, cap=cap, tk=tk)
    return pl.pallas_call(
        kern,
        grid=(b, ne),
        in_specs=[pl.BlockSpec((1, 1, 1, s), lambda bi, e: (bi, e, 0, 0)),
                  pl.BlockSpec((1, 1, 1, s), lambda bi, e: (bi, e, 0, 0)),
                  pl.BlockSpec((1, s, d), lambda bi, e: (bi, 0, 0))],
        out_specs=[pl.BlockSpec((1, 1, cap, d), lambda bi, e: (bi, e, 0, 0)),
                   pl.BlockSpec((1, 1, cap, LANES), lambda bi, e: (bi, e, 0, 0))],
        out_shape=[jax.ShapeDtypeStruct((b, ne, cap, d), BF16),
                   jax.ShapeDtypeStruct((b, ne, cap, LANES), F32)],
        compiler_params=_cparams(("parallel", "arbitrary")),
    )(pos4, aff4, h2)


def _ffn_kernel(xe_ref, gate_ref, wg_ref, wu_ref, wd_ref, ye_ref, wg_s, wu_s, wd_s):
    @pl.when(pl.program_id(1) == 0)
    def _():
        wg_s[...] = wg_ref[0].astype(BF16)
        wu_s[...] = wu_ref[0].astype(BF16)
        wd_s[...] = wd_ref[0].astype(BF16)

    x = xe_ref[0, 0]
    g = _dot(x, wg_s[...])
    u = _dot(x, wu_s[...])
    hid = (g * _sigmoid(g)) * u
    y = _dot(hid.astype(BF16), wd_s[...])
    ye_ref[0, 0] = (y * gate_ref[0, 0][:, 0:1]).astype(BF16)


def _ffn_call(xe, gate, wg, wu, wd):
    b, ne, cap, d = xe.shape
    f = wg.shape[-1]
    return pl.pallas_call(
        _ffn_kernel,
        grid=(ne, b),
        in_specs=[pl.BlockSpec((1, 1, cap, d), lambda e, bi: (bi, e, 0, 0)),
                  pl.BlockSpec((1, 1, cap, LANES), lambda e, bi: (bi, e, 0, 0)),
                  pl.BlockSpec((1, d, f), lambda e, bi: (e, 0, 0)),
                  pl.BlockSpec((1, d, f), lambda e, bi: (e, 0, 0)),
                  pl.BlockSpec((1, f, d), lambda e, bi: (e, 0, 0))],
        out_specs=pl.BlockSpec((1, 1, cap, d), lambda e, bi: (bi, e, 0, 0)),
        out_shape=jax.ShapeDtypeStruct((b, ne, cap, d), BF16),
        scratch_shapes=[pltpu.VMEM((d, f), BF16), pltpu.VMEM((d, f), BF16), pltpu.VMEM((f, d), BF16)],
        compiler_params=_cparams(("parallel", "arbitrary")),
    )(xe, gate, wg, wu, wd)


def _combine_kernel(pos_ref, ye_ref, x1_ref, mod_ref, gfin_ref, o_ref, acc_ref, *, cap, last_norm):
    g = pl.program_id(2)
    eg = ye_ref.shape[1]

    @pl.when(g == 0)
    def _():
        acc_ref[...] = jnp.zeros_like(acc_ref)

    slot = lax.broadcasted_iota(I32, (1, cap), 1)
    pos = pos_ref[0, 0]
    onehot = jnp.concatenate(
        [jnp.where(pos[:, j:j + 1] == slot, 1.0, 0.0).astype(BF16) for j in range(eg)], axis=1)
    acc_ref[...] += _dot(onehot, ye_ref[0].reshape(eg * cap, ye_ref.shape[-1]))

    @pl.when(g == pl.num_programs(2) - 1)
    def _():
        x2 = x1_ref[0] + mod_ref[0, 5:6, :] * acc_ref[...]
        if last_norm:
            x2 = x2 * lax.rsqrt(jnp.mean(x2 * x2, axis=-1, keepdims=True) + EPS) * gfin_ref[...]
        o_ref[0] = x2


def _combine_call(pos_tok, ye, x1, mod3, g_final, cap, last_norm):
    b, s, d = x1.shape
    ne = ye.shape[1]
    n_groups, eg = pos_tok.shape[1], pos_tok.shape[3]
    tm = min(s, 512)
    kern = functools.partial(_combine_kernel, cap=cap, last_norm=last_norm)
    return pl.pallas_call(
        kern,
        grid=(b, s // tm, n_groups),
        in_specs=[pl.BlockSpec((1, 1, tm, eg), lambda bi, i, e: (bi, e, i, 0)),
                  pl.BlockSpec((1, eg, cap, d), lambda bi, i, e: (bi, e, 0, 0)),
                  pl.BlockSpec((1, tm, d), lambda bi, i, e: (bi, i, 0)),
                  pl.BlockSpec((1, 6, d), lambda bi, i, e: (bi, 0, 0)),
                  pl.BlockSpec((1, d), lambda bi, i, e: (0, 0))],
        out_specs=pl.BlockSpec((1, tm, d), lambda bi, i, e: (bi, i, 0)),
        out_shape=jax.ShapeDtypeStruct((b, s, d), F32),
        scratch_shapes=[pltpu.VMEM((tm, d), F32)],
        compiler_params=_cparams(("parallel", "parallel", "arbitrary")),
    )(pos_tok, ye, x1, mod3, g_final.reshape(1, d))


def _pack_w_in(w):
    o_ba = 4 * 1024 + 0
    o_b = o_ba
    o_a = o_b + 2 * DN_HEADS
    o_cq = o_a + 2 * DN_HEADS
    o_ckv = o_cq + Q_LORA
    o_kr = o_ckv + KV_LORA
    o_g = o_kr + ROPE_DIM
    half = ROPE_DIM // 2
    w_kr = w[:, o_kr:o_g]
    w_kr_sw = jnp.concatenate([w_kr[:, half:], w_kr[:, :half]], axis=1)
    pad = jnp.zeros((w.shape[0], COL_GATE - COL_BA - 4 * DN_HEADS), w.dtype)
    packed = jnp.concatenate([w[:, :o_ba], w[:, o_cq:o_ckv], w[:, o_ckv:o_kr], w_kr, w_kr_sw,
                              w[:, o_b:o_a], w[:, o_a:o_cq], pad, w[:, o_g:]], axis=1)
    assert packed.shape[1] == N_PAD
    return packed.astype(BF16)


def _lane_row(vals, offset):
    row = jnp.zeros((1, LANES), F32)
    return row.at[0, offset:offset + vals.size].set(vals.reshape(-1).astype(F32))


def kernel(x, c, positions, w_mod, b_mod, g_mix, w_in, conv_w, a_log, dt_bias, dn_o_gain, q_gain, w_uq,
           kv_gain, w_ukv, w_o_dn, w_o_mla, w_out, g_ffn, w_router, w_gate, w_up, w_down, g_final):
    b, s, d = x.shape
    depth = w_mod.shape[0]
    cap = CAPACITY_FACTOR * s // N_EXPERTS
    half = ROPE_DIM // 2
    inv_freq = ROPE_THETA ** (-jnp.arange(half, dtype=F32) / half)
    invf = jnp.concatenate([inv_freq, inv_freq]).reshape(1, ROPE_DIM)
    sgn = jnp.concatenate([-jnp.ones((half,), F32), jnp.ones((half,), F32)]).reshape(1, ROPE_DIM)
    pos3 = positions.reshape(b, s, 1)
    c_pad = jnp.zeros((8, d), F32).at[:b].set(c)
    swap = np.concatenate([np.arange(half, ROPE_DIM), np.arange(half)])

    for l in range(depth):
        mod3 = _mod_call(c_pad, w_mod[l], b_mod[l])[:b].reshape(b, 6, d)
        proj, ba = _inproj_call(x, mod3, g_mix[l], _pack_w_in(w_in[l]))

        alog_row = _lane_row(a_log[l], 2 * DN_HEADS)
        dt_row = _lane_row(dt_bias[l], 2 * DN_HEADS)
        y_dn = _dn_call(proj, ba, conv_w[l], alog_row, dt_row, dn_o_gain[l])

        wq = w_uq[l].reshape(Q_LORA, MLA_HEADS, QK_DIM)
        wqn = wq[:, :, :NOPE_DIM].reshape(Q_LORA, MLA_HEADS * NOPE_DIM).astype(BF16)
        wqr = wq[:, :, NOPE_DIM:].reshape(Q_LORA, MLA_HEADS * ROPE_DIM).astype(BF16)
        wqs = wq[:, :, NOPE_DIM:][:, :, swap].reshape(Q_LORA, MLA_HEADS * ROPE_DIM).astype(BF16)
        wkv = w_ukv[l].reshape(KV_LORA, MLA_HEADS, NOPE_DIM + V_DIM)
        wkn = wkv[:, :, :NOPE_DIM].reshape(KV_LORA, MLA_HEADS * NOPE_DIM).astype(BF16)
        wv = wkv[:, :, NOPE_DIM:].reshape(KV_LORA, MLA_HEADS * V_DIM).astype(BF16)
        q, k, v = _mlaproj_call(proj, pos3, invf, sgn, q_gain[l], kv_gain[l], wqn, wqr, wqs, wkn, wv)
        y_mla = _attn_call(q, k, v)

        x1, h2, aff = _merge_call(y_dn, y_mla, proj, x, mod3, w_o_dn[l].astype(BF16),
                                  w_o_mla[l].astype(BF16), w_out[l].astype(BF16), g_ffn[l],
                                  w_router[l].T.astype(BF16))
        pos = _route_call(aff, cap)
        pos4 = pos.reshape(b, N_EXPERTS, 1, s)
        aff4 = aff.reshape(b, N_EXPERTS, 1, s)
        xe, gate = _gather_call(pos4, aff4, h2, cap)
        ye = _ffn_call(xe, gate, w_gate[l], w_up[l], w_down[l])
        pos_tok = pos.reshape(b, N_EXPERTS // COMBINE_GROUP, COMBINE_GROUP, s).transpose(0, 1, 3, 2)
        x = _combine_call(pos_tok, ye, x1, mod3, g_final, cap, last_norm=(l == depth - 1))
    return x
```

```python
import functools
import math

import jax
import jax.numpy as jnp
import numpy as np
from jax import lax
from jax.experimental import pallas as pl
from jax.experimental.pallas import tpu as pltpu

F32 = jnp.float32
BF16 = jnp.bfloat16
I32 = jnp.int32

EPS = 1e-6
DN_HEADS = 8
DN_HEAD_DIM = 128
CONV_K = 5
CHUNK = 64
PAIR = 2 * CHUNK
AQ_ROWS = DN_HEAD_DIM + PAIR
MLA_HEADS = 8
NOPE_DIM = 128
ROPE_DIM = 64
V_DIM = 128
QK_DIM = NOPE_DIM + ROPE_DIM
Q_LORA = 512
KV_LORA = 256
ROPE_THETA = 10000.0
N_EXPERTS = 16
CAPACITY_FACTOR = 2
COMBINE_GROUP = 4
LANES = 128
VMEM_LIMIT = 56 * 1024 * 1024

COL_Q, COL_K, COL_V, COL_Z = 0, 1024, 2048, 3072
COL_CQ, COL_CKV, COL_KR, COL_BA, COL_GATE = 4096, 4608, 4864, 4992, 5120
N_PAD = 7168


def _cparams(sem):
    return pltpu.CompilerParams(dimension_semantics=sem, vmem_limit_bytes=VMEM_LIMIT)


def _dot(a, b):
    return jnp.dot(a, b, preferred_element_type=F32)


def _dot_nt(a, b):
    return lax.dot_general(a, b, (((1,), (1,)), ((), ())), preferred_element_type=F32)


def _dot_tn(a, b):
    return lax.dot_general(a, b, (((0,), (0,)), ((), ())), preferred_element_type=F32)


def _sigmoid(x):
    return 1.0 / (1.0 + jnp.exp(-x))


def _softplus(x):
    return jnp.maximum(x, 0.0) + jnp.log(1.0 + jnp.exp(-jnp.abs(x)))


def _split2(x):
    h = x.astype(BF16)
    return h, (x - h.astype(F32)).astype(BF16)


def _split3(x):
    h = x.astype(BF16)
    r = x - h.astype(F32)
    m = r.astype(BF16)
    return h, m, (r - m.astype(F32)).astype(BF16)


def _dot3_parts(a_parts, b_parts):
    ah, al = a_parts
    bh, bl = b_parts
    return _dot(jnp.concatenate([ah, ah, al], axis=1), jnp.concatenate([bh, bl, bh], axis=0))


def _mod_kernel(c_ref, w_ref, b_ref, o_ref):
    c = c_ref[...]
    s = c * _sigmoid(c)
    o_ref[...] = _dot(s.astype(BF16), w_ref[...].astype(BF16)) + b_ref[...]


def _mod_call(c_pad, w_mod, b_mod):
    rows, d = c_pad.shape
    n = w_mod.shape[1]
    tn = 1024
    return pl.pallas_call(
        _mod_kernel,
        grid=(n // tn,),
        in_specs=[pl.BlockSpec((rows, d), lambda j: (0, 0)),
                  pl.BlockSpec((d, tn), lambda j: (0, j)),
                  pl.BlockSpec((1, tn), lambda j: (0, j))],
        out_specs=pl.BlockSpec((rows, tn), lambda j: (0, j)),
        out_shape=jax.ShapeDtypeStruct((rows, n), F32),
        compiler_params=_cparams(("arbitrary",)),
    )(c_pad, w_mod, b_mod.reshape(1, n))


def _inproj_kernel(x_ref, mod_ref, g_ref, w_ref, o_ref, ba_ref, h_scr, *, tn):
    j = pl.program_id(2)

    @pl.when(j == 0)
    def _():
        x = x_ref[0]
        ms = jnp.mean(x * x, axis=-1, keepdims=True)
        y = x * lax.rsqrt(ms + EPS) * g_ref[...]
        h = y * (1.0 + mod_ref[0, 1:2, :]) + mod_ref[0, 0:1, :]
        h_scr[...] = h.astype(BF16)

    acc = _dot(h_scr[...], w_ref[...])
    o_ref[0] = acc.astype(BF16)

    @pl.when(j == COL_BA // tn)
    def _():
        ba_ref[0] = acc[:, COL_BA % tn:COL_BA % tn + LANES]


def _inproj_call(x, mod3, g_mix, w_in_p):
    b, s, d = x.shape
    n = w_in_p.shape[1]
    tm = min(s, 1024)
    tn = 1024
    return pl.pallas_call(
        functools.partial(_inproj_kernel, tn=tn),
        grid=(b, s // tm, n // tn),
        in_specs=[pl.BlockSpec((1, tm, d), lambda bi, i, j: (bi, i, 0)),
                  pl.BlockSpec((1, 6, d), lambda bi, i, j: (bi, 0, 0)),
                  pl.BlockSpec((1, d), lambda bi, i, j: (0, 0)),
                  pl.BlockSpec((d, tn), lambda bi, i, j: (0, j))],
        out_specs=[pl.BlockSpec((1, tm, tn), lambda bi, i, j: (bi, i, j)),
                   pl.BlockSpec((1, tm, LANES), lambda bi, i, j: (bi, i, 0))],
        out_shape=[jax.ShapeDtypeStruct((b, s, n), BF16),
                   jax.ShapeDtypeStruct((b, s, LANES), F32)],
        scratch_shapes=[pltpu.VMEM((tm, d), BF16)],
        compiler_params=_cparams(("parallel", "parallel", "arbitrary")),
    )(x, mod3, g_mix.reshape(1, d), w_in_p)


def _dn_kernel(q_ref, k_ref, v_ref, z_ref, ba_ref, cwq_ref, cwk_ref, cwv_ref, alog_ref, dt_ref, og_ref,
               y_ref,
               xp, qn, kn, vn, aq_s, b_s, egl_s, o_s, *, seq, tr, ppb):
    head = pl.program_id(1)
    n_pairs = seq // PAIR
    n_tiles = seq // tr

    def conv_phase(x_ref, cw_ref, dst, normalise, scale):
        xp[0:8, :] = jnp.zeros((8, LANES), F32)
        xp[seq + 8:seq + 16, :] = jnp.zeros((8, LANES), F32)

        def copy_tile(t, c):
            r0 = pl.multiple_of(t * tr, tr)
            xp[pl.ds(r0 + 8, tr), :] = x_ref[0, pl.ds(r0, tr), :].astype(F32)
            return c

        lax.fori_loop(0, n_tiles, copy_tile, 0)
        cw = cw_ref[...]

        def tile(t, c):
            r0 = pl.multiple_of(t * tr, tr)
            acc = xp[pl.ds(r0 + 6, tr), :] * cw[0:1]
            for kk in range(1, CONV_K):
                acc = acc + xp[pl.ds(r0 + 6 + kk, tr), :] * cw[kk:kk + 1]
            y = acc * _sigmoid(acc)
            if normalise:
                y = y * lax.rsqrt(jnp.sum(y * y, axis=-1, keepdims=True) + EPS)
            if scale != 1.0:
                y = y * scale
            dst[pl.ds(r0, tr), :] = y
            return c

        lax.fori_loop(0, n_tiles, tile, 0)

    conv_phase(q_ref, cwq_ref, qn, True, DN_HEAD_DIM ** -0.5)
    conv_phase(k_ref, cwk_ref, kn, True, 1.0)
    conv_phase(v_ref, cwv_ref, vn, False, 1.0)

    ri = lax.broadcasted_iota(I32, (PAIR, PAIR), 0)
    ci = lax.broadcasted_iota(I32, (PAIR, PAIR), 1)
    same = (ri // CHUNK) == (ci // CHUNK)
    mask_incl = (same & (ci <= ri), same & (ci >= ri))
    mask_strict = (same & (ci < ri), same & (ci > ri))
    eye = jnp.where(ri == ci, 1.0, 0.0).astype(F32)
    first_chunk = ci < CHUNK
    lu = jnp.concatenate([jnp.where(mask_incl[0], 1.0, 0.0), jnp.where(mask_incl[1], 1.0, 0.0)],
                         axis=0).astype(BF16)
    lu3 = jnp.concatenate([lu, lu, lu], axis=1)
    alog_row = alog_ref[...]
    dt_row = dt_ref[...]

    def lane_bcast(x, col):
        shifted = pltpu.roll(x, shift=lax.rem(LANES - col, LANES), axis=1)
        return jnp.broadcast_to(shifted[:, 0:1], x.shape)

    def wide(x):
        return jnp.concatenate([x, x], axis=1)

    same16 = wide((ri // 16) == (ci // 16))
    same32 = wide((ri // 32) == (ci // 32))
    off16 = same32 & jnp.logical_not(same16)
    eye_w = wide(eye)
    zero_blk = jnp.zeros((PAIR, PAIR), BF16)

    def bdiag(w):
        return jnp.concatenate([jnp.concatenate([w[:, :PAIR], zero_blk], axis=1),
                                jnp.concatenate([zero_blk, w[:, PAIR:]], axis=1)], axis=0)

    def tri_inverse_many(lws):
        d0s = [jnp.where(same16, lw, 0.0) for lw in lws]
        ms = [(-d0).astype(BF16) for d0 in d0s]
        xs = [eye_w - d0 for d0 in d0s]
        for _ in range(3):
            ms = [_dot(m, bdiag(m)).astype(BF16) for m in ms]
            xs = [x + _dot(x.astype(BF16), bdiag(m)) for x, m in zip(xs, ms)]
        for level in range(2):
            cs = [bdiag((jnp.where(off16, lw, 0.0) if level == 0 else jnp.where(same32, 0.0, lw)).astype(BF16))
                  for lw in lws]
            xbs = [x.astype(BF16) for x in xs]
            ts = [_dot(xb, c).astype(BF16) for xb, c in zip(xbs, cs)]
            xs = [x - _dot(t, bdiag(xb)) for x, t, xb in zip(xs, ts, xbs)]
        rs = []
        for lw, x in zip(lws, xs):
            lh, ll = _split2(lw)
            xh, xl = _split2(x)
            prod = _dot(jnp.concatenate([lh, lh, ll], axis=1),
                        jnp.concatenate([bdiag(xh), bdiag(xl), bdiag(xh)], axis=0))
            rs.append((eye_w - x) - prod)
        return [x + _dot(x.astype(BF16), bdiag(r.astype(BF16))) for x, r in zip(xs, rs)]

    def gate_stage(pairs):
        r0s = [pl.multiple_of(p * PAIR, PAIR) for p in pairs]
        blks = [ba_ref[0, pl.ds(r0, PAIR), :] for r0 in r0s]
        gs = [-jnp.exp(alog_row) * _softplus(blk + dt_row) for blk in blks]
        betas = [jnp.concatenate([lane_bcast(sg, head), lane_bcast(sg, DN_HEADS + head)], axis=1)
                 for sg in [_sigmoid(blk) for blk in blks]]
        g_reps = [jnp.concatenate([lane_bcast(g, 2 * DN_HEADS + head), lane_bcast(g, 3 * DN_HEADS + head)],
                                  axis=1) for g in gs]
        gps = [jnp.concatenate(_split3(g_rep), axis=0) for g_rep in g_reps]
        kqs = []
        for r0 in r0s:
            kb = kn[pl.ds(r0, PAIR), :].astype(BF16)
            kqs.append(_dot_nt(jnp.concatenate([kb, qn[pl.ds(r0, PAIR), :].astype(BF16)], axis=0), kb))
        css = [_dot(lu3, gp) for gp in gps]
        out = []
        for r0, beta, g_rep, cs, kq in zip(r0s, betas, g_reps, css, kqs):
            pre, suf = cs[:PAIR], cs[PAIR:]
            per_dir = ((beta[:, :LANES], pre[:, :LANES], suf[:, :LANES] - g_rep[:, :LANES]),
                       (beta[:, LANES:], suf[:, LANES:], pre[:, LANES:] - g_rep[:, LANES:]))
            out.append((r0, per_dir, kq[:PAIR], kq[PAIR:]))
        return out

    def finish_stage(chains, t_invs):
        egs, wus = [], []
        for (p, r0, d, beta, gc, ex, qk, dec), t_inv in zip(chains, t_invs):
            k2 = kn[pl.ds(r0, PAIR), :]
            eg = jnp.exp(gc)
            vb = (vn[pl.ds(r0, PAIR), :] * beta).astype(BF16)
            kbg = (k2 * beta * eg).astype(BF16)
            egs.append(eg)
            wus.append(_dot(t_inv.astype(BF16), jnp.concatenate([kbg, vb], axis=1)).astype(BF16))
        abs_, qos = [], []
        for (p, r0, d, beta, gc, ex, qk, dec), wu in zip(chains, wus):
            kst = (kn[pl.ds(r0, PAIR), :] * jnp.exp(ex)).T.astype(BF16)
            kst2 = jnp.concatenate([jnp.where(first_chunk, kst, 0.0), jnp.where(first_chunk, 0.0, kst)],
                                   axis=0).astype(BF16)
            abs_.append(_dot(kst2, wu))
            intra = jnp.where(mask_incl[d], qk * dec, 0.0).astype(BF16)
            qos.append(_dot(intra, wu))
        parts, comps = [], []
        for (p, r0, d, beta, gc, ex, qk, dec), eg, ab, qo in zip(chains, egs, abs_, qos):
            c1, c2 = (0, 1) if d == 0 else (1, 0)
            a = (-ab[:PAIR, :LANES], -ab[PAIR:, :LANES])
            b = (ab[:PAIR, LANES:], ab[PAIR:, LANES:])
            qp = qn[pl.ds(r0, PAIR), :] * eg - qo[:, :LANES]
            q = (qp[:CHUNK], qp[CHUNK:])
            comps.append(_dot(jnp.concatenate([a[c2], q[c2]], axis=0).astype(BF16),
                              jnp.concatenate([a[c1], b[c1]], axis=1).astype(BF16)))
            parts.append((a, b, q, c1, c2))
        for (p, r0, d, beta, gc, ex, qk, dec), qo, (a, b, q, c1, c2), comp in zip(chains, qos, parts, comps):
            etot = jnp.exp(gc + ex)
            e1, e2 = etot[c1 * CHUNK:c1 * CHUNK + 1], etot[c2 * CHUNK:c2 * CHUNK + 1]
            m = e2 * a[c1] + e1 * a[c2] + comp[:PAIR, :LANES]
            a0 = pl.multiple_of(p * AQ_ROWS, AQ_ROWS)
            aq_s[d, pl.ds(a0, PAIR), :] = m.astype(BF16)
            aq_s[d, pl.ds(a0 + PAIR, CHUNK), :] = q[c1].astype(BF16)
            aq_s[d, pl.ds(a0 + PAIR + CHUNK, CHUNK), :] = (e1 * q[c2] + comp[PAIR:, :LANES]).astype(BF16)
            b_s[d, pl.ds(r0, PAIR), :] = e2 * b[c1] + comp[:PAIR, LANES:] + b[c2]
            o1 = pl.ds(r0 + c1 * CHUNK, CHUNK)
            o2 = pl.ds(r0 + c2 * CHUNK, CHUNK)
            o_s[o1, :] = o_s[o1, :] + qo[c1 * CHUNK:(c1 + 1) * CHUNK, LANES:]
            o_s[o2, :] = o_s[o2, :] + qo[c2 * CHUNK:(c2 + 1) * CHUNK, LANES:] + comp[PAIR:, LANES:]
            e0 = pl.multiple_of(p * 8, 8)
            egl_s[d, pl.ds(e0, 8), :] = etot[0:8] * etot[CHUNK:CHUNK + 8]

    def prep_block(i, c):
        pairs = [i * ppb + j for j in range(ppb)]
        chains, lws = [], []
        for p, (r0, per_dir, kk, qk) in zip(pairs, gate_stage(pairs)):
            lms = []
            for d in range(2):
                beta, gc, ex = per_dir[d]
                m_in = mask_incl[d]
                dec = jnp.where(m_in, jnp.exp(jnp.where(m_in, gc - gc.T, 0.0)), 0.0)
                lms.append(jnp.where(mask_strict[d], beta * kk * dec, 0.0))
                chains.append((p, r0, d, beta, gc, ex, qk, dec))
            lws.append(jnp.concatenate(lms, axis=1))
        t_invs = []
        for t_w in tri_inverse_many(lws):
            t_invs += [t_w[:, :PAIR], t_w[:, PAIR:]]
        finish_stage(chains, t_invs)
        return c

    def zero_tile(t, c):
        r0 = pl.multiple_of(t * tr, tr)
        o_s[pl.ds(r0, tr), :] = jnp.zeros((tr, LANES), F32)
        return c

    lax.fori_loop(0, n_tiles, zero_tile, 0)
    lax.fori_loop(0, n_pairs // ppb, prep_block, 0)

    def pair_step(d, pair, state):
        c1 = d
        r0 = pl.multiple_of(pair * PAIR, PAIR)
        a0 = pl.multiple_of(pair * AQ_ROWS, AQ_ROWS)
        res = _dot(aq_s[d, pl.ds(a0, AQ_ROWS), :], state.astype(BF16))
        o1 = pl.ds(r0 + c1 * CHUNK, CHUNK)
        o2 = pl.ds(r0 + (1 - c1) * CHUNK, CHUNK)
        o_s[o1, :] = o_s[o1, :] + res[PAIR:PAIR + CHUNK]
        o_s[o2, :] = o_s[o2, :] + res[PAIR + CHUNK:]
        eg = egl_s[d, pl.ds(pl.multiple_of(pair * 8, 8), 8), :][0:1, :]
        return state * eg + res[:PAIR] + b_s[d, pl.ds(r0, PAIR), :]

    def pair_scan(n, carry):
        s_f, s_b = carry
        return pair_step(0, n, s_f), pair_step(1, n_pairs - 1 - n, s_b)

    zero_state = jnp.zeros((DN_HEAD_DIM, DN_HEAD_DIM), F32)
    lax.fori_loop(0, n_pairs, pair_scan, (zero_state, zero_state))

    og = og_ref[...]

    def out_tile(t, c):
        r0 = pl.multiple_of(t * tr, tr)
        o = o_s[pl.ds(r0, tr), :]
        z = z_ref[0, pl.ds(r0, tr), :].astype(F32)
        y = o * lax.rsqrt(jnp.mean(o * o, axis=-1, keepdims=True) + EPS) * og
        y_ref[0, pl.ds(r0, tr), :] = (y * (z * _sigmoid(z))).astype(BF16)
        return c

    lax.fori_loop(0, n_tiles, out_tile, 0)


def _dn_call(proj, ba, conv_w, alog_row, dt_row, o_gain):
    b, s, _ = proj.shape
    hd = DN_HEAD_DIM
    tr = min(s, 1024)
    qb, kb, vb, zb = COL_Q // hd, COL_K // hd, COL_V // hd, COL_Z // hd

    def col(base):
        return pl.BlockSpec((1, s, hd), lambda bi, h: (bi, 0, base + h))

    def cw(base):
        return pl.BlockSpec((CONV_K, hd), lambda bi, h: (0, base + h))

    row = pl.BlockSpec((1, LANES), lambda bi, h: (0, 0))
    n_pairs = s // PAIR
    ppb = next(n for n in (8, 4, 2, 1) if n_pairs % n == 0)
    kern = functools.partial(_dn_kernel, seq=s, tr=tr, ppb=ppb)
    return pl.pallas_call(
        kern,
        grid=(b, DN_HEADS),
        in_specs=[col(qb), col(kb), col(vb), col(zb),
                  pl.BlockSpec((1, s, LANES), lambda bi, h: (bi, 0, 0)),
                  cw(qb), cw(kb), cw(vb), row, row, row],
        out_specs=pl.BlockSpec((1, s, hd), lambda bi, h: (bi, 0, h)),
        out_shape=jax.ShapeDtypeStruct((b, s, DN_HEADS * hd), BF16),
        scratch_shapes=[pltpu.VMEM((s + 16, LANES), F32),
                        pltpu.VMEM((s, hd), F32), pltpu.VMEM((s, hd), F32), pltpu.VMEM((s, hd), F32),
                        pltpu.VMEM((2, s // PAIR * AQ_ROWS, hd), BF16),
                        pltpu.VMEM((2, s // PAIR * hd, hd), F32),
                        pltpu.VMEM((2, s // PAIR * 8, LANES), F32),
                        pltpu.VMEM((s, hd), F32)],
        compiler_params=_cparams(("parallel", "parallel")),
    )(proj, proj, proj, proj, ba, conv_w, conv_w, conv_w, alog_row, dt_row, o_gain.reshape(1, hd))


def _mlaproj_kernel(cq_ref, ckv_ref, kr_ref, pos_ref, invf_ref, sgn_ref, qg_ref, kvg_ref,
                    wqn_ref, wqr_ref, wqs_ref, wkn_ref, wv_ref,
                    q_ref, k_ref, v_ref):
    cq = cq_ref[0].astype(F32)
    hq = (cq * lax.rsqrt(jnp.mean(cq * cq, axis=-1, keepdims=True) + EPS) * qg_ref[...]).astype(BF16)
    qn = _dot(hq, wqn_ref[...])
    qr = _dot(hq, wqr_ref[...])
    qs = _dot(hq, wqs_ref[...])
    ang = pos_ref[0].astype(F32) * invf_ref[...]
    cc = jnp.cos(ang)
    ss = jnp.sin(ang) * sgn_ref[...]
    scale = QK_DIM ** -0.5 * math.log2(math.e)
    ckv = ckv_ref[0].astype(F32)
    hkv = (ckv * lax.rsqrt(jnp.mean(ckv * ckv, axis=-1, keepdims=True) + EPS) * kvg_ref[...]).astype(BF16)
    kn = _dot(hkv, wkn_ref[...])
    vv = _dot(hkv, wv_ref[...])
    kr = kr_ref[0].astype(F32)
    k_rope = (kr[:, :ROPE_DIM] * cc + kr[:, ROPE_DIM:] * ss).astype(BF16)
    for h in range(MLA_HEADS):
        lo, hi = h * ROPE_DIM, (h + 1) * ROPE_DIM
        q_rope = qr[:, lo:hi] * cc + qs[:, lo:hi] * ss
        q_ref[0, h, :, 0:NOPE_DIM] = (qn[:, h * NOPE_DIM:(h + 1) * NOPE_DIM] * scale).astype(BF16)
        q_ref[0, h, :, NOPE_DIM:QK_DIM] = (q_rope * scale).astype(BF16)
        k_ref[0, h, :, 0:NOPE_DIM] = kn[:, h * NOPE_DIM:(h + 1) * NOPE_DIM].astype(BF16)
        k_ref[0, h, :, NOPE_DIM:QK_DIM] = k_rope
        v_ref[0, h] = vv[:, h * V_DIM:(h + 1) * V_DIM].astype(BF16)


def _mlaproj_call(proj, pos3, invf, sgn, q_gain, kv_gain, wqn, wqr, wqs, wkn, wv):
    b, s, _ = proj.shape
    tm = min(s, 512)
    nh = MLA_HEADS

    def full(a):
        return pl.BlockSpec(a.shape, lambda bi, i: (0,) * a.ndim)

    qg = q_gain.reshape(1, Q_LORA)
    kvg = kv_gain.reshape(1, KV_LORA)
    return pl.pallas_call(
        _mlaproj_kernel,
        grid=(b, s // tm),
        in_specs=[pl.BlockSpec((1, tm, Q_LORA), lambda bi, i: (bi, i, COL_CQ // Q_LORA)),
                  pl.BlockSpec((1, tm, KV_LORA), lambda bi, i: (bi, i, COL_CKV // KV_LORA)),
                  pl.BlockSpec((1, tm, LANES), lambda bi, i: (bi, i, COL_KR // LANES)),
                  pl.BlockSpec((1, tm, 1), lambda bi, i: (bi, i, 0)),
                  full(invf), full(sgn), full(qg), full(kvg),
                  full(wqn), full(wqr), full(wqs), full(wkn), full(wv)],
        out_specs=[pl.BlockSpec((1, nh, tm, QK_DIM), lambda bi, i: (bi, 0, i, 0)),
                   pl.BlockSpec((1, nh, tm, QK_DIM), lambda bi, i: (bi, 0, i, 0)),
                   pl.BlockSpec((1, nh, tm, V_DIM), lambda bi, i: (bi, 0, i, 0))],
        out_shape=[jax.ShapeDtypeStruct((b, nh, s, QK_DIM), BF16),
                   jax.ShapeDtypeStruct((b, nh, s, QK_DIM), BF16),
                   jax.ShapeDtypeStruct((b, nh, s, V_DIM), BF16)],
        compiler_params=_cparams(("parallel", "parallel")),
    )(proj, proj, proj, pos3, invf, sgn, qg, kvg, wqn, wqr, wqs, wkn, wv)


def _attn_kernel(q_ref, k_ref, v_ref, o_ref, *, seq, tq, sub, tk):
    n_sub, n_c = tq // sub, seq // tk

    def scores(a, c):
        return _dot_nt(q_ref[0, 0, a * sub:(a + 1) * sub, :], k_ref[0, 0, c * tk:(c + 1) * tk, :])

    s_cur = [scores(0, c) for c in range(n_c)]
    for a in range(n_sub):
        mx = s_cur[0]
        for c in range(1, n_c):
            mx = jnp.maximum(mx, s_cur[c])
        m = jnp.max(mx, axis=-1, keepdims=True)
        lp = jnp.zeros((sub, LANES), F32)
        acc = jnp.zeros((sub, V_DIM), F32)
        s_next = []
        for c in range(n_c):
            p = jnp.exp2(s_cur[c] - m)
            for j in range(tk // LANES):
                lp = lp + p[:, j * LANES:(j + 1) * LANES]
            acc = acc + _dot(p.astype(BF16), v_ref[0, 0, c * tk:(c + 1) * tk, :])
            if a + 1 < n_sub:
                s_next.append(scores(a + 1, c))
        l = jnp.sum(lp, axis=-1, keepdims=True)
        o_ref[0, a * sub:(a + 1) * sub, :] = (acc / l).astype(BF16)
        s_cur = s_next


def _attn_call(q, k, v):
    b, nh, s, _ = q.shape
    tq = min(s, 2048)
    kern = functools.partial(_attn_kernel, seq=s, tq=tq, sub=min(tq, 256), tk=min(s, 512))
    return pl.pallas_call(
        kern,
        grid=(b, nh, s // tq),
        in_specs=[pl.BlockSpec((1, 1, tq, QK_DIM), lambda bi, h, i: (bi, h, i, 0)),
                  pl.BlockSpec((1, 1, s, QK_DIM), lambda bi, h, i: (bi, h, 0, 0)),
                  pl.BlockSpec((1, 1, s, V_DIM), lambda bi, h, i: (bi, h, 0, 0))],
        out_specs=pl.BlockSpec((1, tq, V_DIM), lambda bi, h, i: (bi, i, h)),
        out_shape=jax.ShapeDtypeStruct((b, s, nh * V_DIM), BF16),
        compiler_params=_cparams(("parallel", "parallel", "arbitrary")),
    )(q, k, v)


def _merge_kernel(ydn_ref, ymla_ref, gdn_ref, gmla_ref, x_ref, mod_ref, wodn_ref, womla_ref, wout_ref,
                  gffn_ref, wrt_ref,
                  x1_ref, h2_ref, aff_ref):
    y_dn = _dot(ydn_ref[0], wodn_ref[...])
    y_mla = _dot(ymla_ref[0], womla_ref[...])
    merged = _sigmoid(gdn_ref[0].astype(F32)) * y_dn + _sigmoid(gmla_ref[0].astype(F32)) * y_mla
    x1 = x_ref[0] + mod_ref[0, 2:3, :] * _dot(merged.astype(BF16), wout_ref[...])
    x1_ref[0] = x1
    y = x1 * lax.rsqrt(jnp.mean(x1 * x1, axis=-1, keepdims=True) + EPS) * gffn_ref[...]
    h2 = (y * (1.0 + mod_ref[0, 4:5, :]) + mod_ref[0, 3:4, :]).astype(BF16)
    h2_ref[0] = h2
    logits = _dot_nt(wrt_ref[...], h2)
    mx = jnp.max(logits, axis=0, keepdims=True)
    ex = jnp.exp(logits - mx)
    aff_ref[0] = ex / jnp.sum(ex, axis=0, keepdims=True)


def _merge_call(y_dn, y_mla, proj, x, mod3, w_o_dn, w_o_mla, w_out, g_ffn, w_router_t):
    b, s, d = x.shape
    tm = min(s, 512)
    ne = w_router_t.shape[0]

    def tok(width):
        return pl.BlockSpec((1, tm, width), lambda bi, i: (bi, i, 0))

    def full(a):
        return pl.BlockSpec(a.shape, lambda bi, i: (0,) * a.ndim)

    gf = g_ffn.reshape(1, d)
    return pl.pallas_call(
        _merge_kernel,
        grid=(b, s // tm),
        in_specs=[tok(d), tok(d),
                  pl.BlockSpec((1, tm, d), lambda bi, i: (bi, i, COL_GATE // d)),
                  pl.BlockSpec((1, tm, d), lambda bi, i: (bi, i, COL_GATE // d + 1)),
                  tok(d),
                  pl.BlockSpec((1, 6, d), lambda bi, i: (bi, 0, 0)),
                  full(w_o_dn), full(w_o_mla), full(w_out), full(gf), full(w_router_t)],
        out_specs=[tok(d), tok(d), pl.BlockSpec((1, ne, tm), lambda bi, i: (bi, 0, i))],
        out_shape=[jax.ShapeDtypeStruct((b, s, d), F32),
                   jax.ShapeDtypeStruct((b, s, d), BF16),
                   jax.ShapeDtypeStruct((b, ne, s), F32)],
        compiler_params=_cparams(("parallel", "parallel")),
    )(y_dn, y_mla, proj, proj, x, mod3, w_o_dn, w_o_mla, w_out, gf, w_router_t)


def _route_kernel(aff_ref, pos_ref, *, seq, cap):
    a = aff_ref[0]
    ne = a.shape[0]
    capf = jnp.float32(cap)

    def count_ge(t_bits):
        t = pltpu.bitcast(t_bits, F32)
        return jnp.sum(jnp.where(a >= t, 1.0, 0.0), axis=1, keepdims=True)

    def search(_, carry):
        lo, hi = carry
        mid = lo + ((hi - lo + 1) >> 1)
        ok = count_ge(mid) >= capf
        return jnp.where(ok, mid, lo), jnp.where(ok, hi, mid - 1)

    lo0 = jnp.zeros((ne, 1), I32)
    hi0 = jnp.full((ne, 1), 0x7F800000, I32)
    thr_bits, _ = lax.fori_loop(0, 32, search, (lo0, hi0))
    thr = pltpu.bitcast(thr_bits, F32)

    ur = lax.broadcasted_iota(I32, (LANES, LANES), 0)
    uc = lax.broadcasted_iota(I32, (LANES, LANES), 1)
    upper = jnp.where(ur <= uc, 1.0, 0.0).astype(BF16)

    def prefix_incl(mask):
        carry = jnp.zeros((ne, 1), F32)
        outs = []
        for j in range(seq // LANES):
            pr = _dot(mask[:, j * LANES:(j + 1) * LANES].astype(BF16), upper) + carry
            outs.append(pr)
            carry = pr[:, LANES - 1:LANES]
        return jnp.concatenate(outs, axis=1)

    gt = a > thr
    eq = a == thr
    n_gt = jnp.sum(jnp.where(gt, 1.0, 0.0), axis=1, keepdims=True)
    take_eq = eq & (prefix_incl(jnp.where(eq, 1.0, 0.0)) <= capf - n_gt)
    sel = gt | take_eq
    slot = prefix_incl(jnp.where(sel, 1.0, 0.0)) - 1.0
    pos_ref[0] = jnp.where(sel, slot, -1.0).astype(I32)


def _route_call(aff, cap):
    b, ne, s = aff.shape
    kern = functools.partial(_route_kernel, seq=s, cap=cap)
    return pl.pallas_call(
        kern,
        grid=(b,),
        in_specs=[pl.BlockSpec((1, ne, s), lambda bi: (bi, 0, 0))],
        out_specs=pl.BlockSpec((1, ne, s), lambda bi: (bi, 0, 0)),
        out_shape=jax.ShapeDtypeStruct((b, ne, s), I32),
        compiler_params=_cparams(("parallel",)),
    )(aff)


def _gather_kernel(pos_ref, aff_ref, h_ref, xe_ref, gate_ref, *, seq, cap, tk):
    slot = lax.broadcasted_iota(I32, (cap, 1), 0)
    acc = jnp.zeros((cap, h_ref.shape[-1]), F32)
    gacc = jnp.zeros((cap, 1), F32)
    for kt in range(seq // tk):
        pm = pos_ref[0, 0, :, kt * tk:(kt + 1) * tk]
        hit = pm == slot
        acc = acc + _dot(jnp.where(hit, 1.0, 0.0).astype(BF16), h_ref[0, kt * tk:(kt + 1) * tk, :])
        am = aff_ref[0, 0, :, kt * tk:(kt + 1) * tk]
        gacc = gacc + jnp.sum(jnp.where(hit, am, 0.0), axis=1, keepdims=True)
    xe_ref[0, 0] = acc.astype(BF16)
    gate_ref[0, 0] = jnp.broadcast_to(gacc, (cap, LANES))


def _gather_call(pos4, aff4, h2, cap):
    b, ne, _, s = pos4.shape
    d = h2.shape[-1]
    tk = min(s, 512)
    kern = functools.partial(_gather_kernel, seq=s, cap=cap, tk=tk)
    return pl.pallas_call(
        kern,
        grid=(b, ne),
        in_specs=[pl.BlockSpec((1, 1, 1, s), lambda bi, e: (bi, e, 0, 0)),
                  pl.BlockSpec((1, 1, 1, s), lambda bi, e: (bi, e, 0, 0)),
                  pl.BlockSpec((1, s, d), lambda bi, e: (bi, 0, 0))],
        out_specs=[pl.BlockSpec((1, 1, cap, d), lambda bi, e: (bi, e, 0, 0)),
                   pl.BlockSpec((1, 1, cap, LANES), lambda bi, e: (bi, e, 0, 0))],
        out_shape=[jax.ShapeDtypeStruct((b, ne, cap, d), BF16),
                   jax.ShapeDtypeStruct((b, ne, cap, LANES), F32)],
        compiler_params=_cparams(("parallel", "arbitrary")),
    )(pos4, aff4, h2)


def _ffn_kernel(xe_ref, gate_ref, wg_ref, wu_ref, wd_ref, ye_ref):
    x = xe_ref[0, 0]
    g = _dot(x, wg_ref[0].astype(BF16))
    u = _dot(x, wu_ref[0].astype(BF16))
    hid = (g * _sigmoid(g)) * u
    y = _dot(hid.astype(BF16), wd_ref[0].astype(BF16))
    ye_ref[0, 0] = (y * gate_ref[0, 0][:, 0:1]).astype(BF16)


def _ffn_call(xe, gate, wg, wu, wd):
    b, ne, cap, d = xe.shape
    f = wg.shape[-1]
    return pl.pallas_call(
        _ffn_kernel,
        grid=(ne, b),
        in_specs=[pl.BlockSpec((1, 1, cap, d), lambda e, bi: (bi, e, 0, 0)),
                  pl.BlockSpec((1, 1, cap, LANES), lambda e, bi: (bi, e, 0, 0)),
                  pl.BlockSpec((1, d, f), lambda e, bi: (e, 0, 0)),
                  pl.BlockSpec((1, d, f), lambda e, bi: (e, 0, 0)),
                  pl.BlockSpec((1, f, d), lambda e, bi: (e, 0, 0))],
        out_specs=pl.BlockSpec((1, 1, cap, d), lambda e, bi: (bi, e, 0, 0)),
        out_shape=jax.ShapeDtypeStruct((b, ne, cap, d), BF16),
        compiler_params=_cparams(("parallel", "arbitrary")),
    )(xe, gate, wg, wu, wd)


def _combine_kernel(pos_ref, ye_ref, x1_ref, mod_ref, gfin_ref, o_ref, acc_ref, *, cap, last_norm):
    g = pl.program_id(2)
    eg = ye_ref.shape[1]

    @pl.when(g == 0)
    def _():
        acc_ref[...] = jnp.zeros_like(acc_ref)

    slot = lax.broadcasted_iota(I32, (1, cap), 1)
    pos = pos_ref[0, 0]
    onehot = jnp.concatenate(
        [jnp.where(pos[:, j:j + 1] == slot, 1.0, 0.0).astype(BF16) for j in range(eg)], axis=1)
    acc_ref[...] += _dot(onehot, ye_ref[0].reshape(eg * cap, ye_ref.shape[-1]))

    @pl.when(g == pl.num_programs(2) - 1)
    def _():
        x2 = x1_ref[0] + mod_ref[0, 5:6, :] * acc_ref[...]
        if last_norm:
            x2 = x2 * lax.rsqrt(jnp.mean(x2 * x2, axis=-1, keepdims=True) + EPS) * gfin_ref[...]
        o_ref[0] = x2


def _combine_call(pos_tok, ye, x1, mod3, g_final, cap, last_norm):
    b, s, d = x1.shape
    ne = ye.shape[1]
    n_groups, eg = pos_tok.shape[1], pos_tok.shape[3]
    tm = min(s, 512)
    kern = functools.partial(_combine_kernel, cap=cap, last_norm=last_norm)
    return pl.pallas_call(
        kern,
        grid=(b, s // tm, n_groups),
        in_specs=[pl.BlockSpec((1, 1, tm, eg), lambda bi, i, e: (bi, e, i, 0)),
                  pl.BlockSpec((1, eg, cap, d), lambda bi, i, e: (bi, e, 0, 0)),
                  pl.BlockSpec((1, tm, d), lambda bi, i, e: (bi, i, 0)),
                  pl.BlockSpec((1, 6, d), lambda bi, i, e: (bi, 0, 0)),
                  pl.BlockSpec((1, d), lambda bi, i, e: (0, 0))],
        out_specs=pl.BlockSpec((1, tm, d), lambda bi, i, e: (bi, i, 0)),
        out_shape=jax.ShapeDtypeStruct((b, s, d), F32),
        scratch_shapes=[pltpu.VMEM((tm, d), F32)],
        compiler_params=_cparams(("parallel", "parallel", "arbitrary")),
    )(pos_tok, ye, x1, mod3, g_final.reshape(1, d))


def _pack_w_in(w):
    o_ba = 4 * 1024 + 0
    o_b = o_ba
    o_a = o_b + 2 * DN_HEADS
    o_cq = o_a + 2 * DN_HEADS
    o_ckv = o_cq + Q_LORA
    o_kr = o_ckv + KV_LORA
    o_g = o_kr + ROPE_DIM
    half = ROPE_DIM // 2
    w_kr = w[:, o_kr:o_g]
    w_kr_sw = jnp.concatenate([w_kr[:, half:], w_kr[:, :half]], axis=1)
    pad = jnp.zeros((w.shape[0], COL_GATE - COL_BA - 4 * DN_HEADS), w.dtype)
    packed = jnp.concatenate([w[:, :o_ba], w[:, o_cq:o_ckv], w[:, o_ckv:o_kr], w_kr, w_kr_sw,
                              w[:, o_b:o_a], w[:, o_a:o_cq], pad, w[:, o_g:]], axis=1)
    assert packed.shape[1] == N_PAD
    return packed.astype(BF16)


def _lane_row(vals, offset):
    row = jnp.zeros((1, LANES), F32)
    return row.at[0, offset:offset + vals.size].set(vals.reshape(-1).astype(F32))


def kernel(x, c, positions, w_mod, b_mod, g_mix, w_in, conv_w, a_log, dt_bias, dn_o_gain, q_gain, w_uq,
           kv_gain, w_ukv, w_o_dn, w_o_mla, w_out, g_ffn, w_router, w_gate, w_up, w_down, g_final):
    b, s, d = x.shape
    depth = w_mod.shape[0]
    cap = CAPACITY_FACTOR * s // N_EXPERTS
    half = ROPE_DIM // 2
    inv_freq = ROPE_THETA ** (-jnp.arange(half, dtype=F32) / half)
    invf = jnp.concatenate([inv_freq, inv_freq]).reshape(1, ROPE_DIM)
    sgn = jnp.concatenate([-jnp.ones((half,), F32), jnp.ones((half,), F32)]).reshape(1, ROPE_DIM)
    pos3 = positions.reshape(b, s, 1)
    c_pad = jnp.zeros((8, d), F32).at[:b].set(c)
    swap = np.concatenate([np.arange(half, ROPE_DIM), np.arange(half)])

    for l in range(depth):
        mod3 = _mod_call(c_pad, w_mod[l], b_mod[l])[:b].reshape(b, 6, d)
        proj, ba = _inproj_call(x, mod3, g_mix[l], _pack_w_in(w_in[l]))

        alog_row = _lane_row(a_log[l], 2 * DN_HEADS)
        dt_row = _lane_row(dt_bias[l], 2 * DN_HEADS)
        y_dn = _dn_call(proj, ba, conv_w[l], alog_row, dt_row, dn_o_gain[l])

        wq = w_uq[l].reshape(Q_LORA, MLA_HEADS, QK_DIM)
        wqn = wq[:, :, :NOPE_DIM].reshape(Q_LORA, MLA_HEADS * NOPE_DIM).astype(BF16)
        wqr = wq[:, :, NOPE_DIM:].reshape(Q_LORA, MLA_HEADS * ROPE_DIM).astype(BF16)
        wqs = wq[:, :, NOPE_DIM:][:, :, swap].reshape(Q_LORA, MLA_HEADS * ROPE_DIM).astype(BF16)
        wkv = w_ukv[l].reshape(KV_LORA, MLA_HEADS, NOPE_DIM + V_DIM)
        wkn = wkv[:, :, :NOPE_DIM].reshape(KV_LORA, MLA_HEADS * NOPE_DIM).astype(BF16)
        wv = wkv[:, :, NOPE_DIM:].reshape(KV_LORA, MLA_HEADS * V_DIM).astype(BF16)
        q, k, v = _mlaproj_call(proj, pos3, invf, sgn, q_gain[l], kv_gain[l], wqn, wqr, wqs, wkn, wv)
        y_mla = _attn_call(q, k, v)

        x1, h2, aff = _merge_call(y_dn, y_mla, proj, x, mod3, w_o_dn[l].astype(BF16),
                                  w_o_mla[l].astype(BF16), w_out[l].astype(BF16), g_ffn[l],
                                  w_router[l].T.astype(BF16))
        pos = _route_call(aff, cap)
        pos4 = pos.reshape(b, N_EXPERTS, 1, s)
        aff4 = aff.reshape(b, N_EXPERTS, 1, s)
        xe, gate = _gather_call(pos4, aff4, h2, cap)
        ye = _ffn_call(xe, gate, w_gate[l], w_up[l], w_down[l])
        pos_tok = pos.reshape(b, N_EXPERTS // COMBINE_GROUP, COMBINE_GROUP, s).transpose(0, 1, 3, 2)
        x = _combine_call(pos_tok, ye, x1, mod3, g_final, cap, last_norm=(l == depth - 1))
    return x
```

```python
import functools
import math

import jax
import jax.numpy as jnp
import numpy as np
from jax import lax
from jax.experimental import pallas as pl
from jax.experimental.pallas import tpu as pltpu

F32 = jnp.float32
BF16 = jnp.bfloat16
I32 = jnp.int32

EPS = 1e-6
DN_HEADS = 8
DN_HEAD_DIM = 128
CONV_K = 5
CHUNK = 64
PAIR = 2 * CHUNK
AQ_ROWS = DN_HEAD_DIM + PAIR
MLA_HEADS = 8
NOPE_DIM = 128
ROPE_DIM = 64
V_DIM = 128
QK_DIM = NOPE_DIM + ROPE_DIM
Q_LORA = 512
KV_LORA = 256
ROPE_THETA = 10000.0
N_EXPERTS = 16
CAPACITY_FACTOR = 2
COMBINE_GROUP = 8
LANES = 128
VMEM_LIMIT = 56 * 1024 * 1024

COL_Q, COL_K, COL_V, COL_Z = 0, 1024, 2048, 3072
COL_CQ, COL_CKV, COL_KR, COL_BA, COL_GATE = 4096, 4608, 4864, 4992, 5120
N_PAD = 7168


def _cparams(sem):
    return pltpu.CompilerParams(dimension_semantics=sem, vmem_limit_bytes=VMEM_LIMIT)


def _dot(a, b):
    return jnp.dot(a, b, preferred_element_type=F32)


def _dot_nt(a, b):
    return lax.dot_general(a, b, (((1,), (1,)), ((), ())), preferred_element_type=F32)


def _dot_tn(a, b):
    return lax.dot_general(a, b, (((0,), (0,)), ((), ())), preferred_element_type=F32)


def _sigmoid(x):
    return 1.0 / (1.0 + jnp.exp(-x))


def _softplus(x):
    return jnp.maximum(x, 0.0) + jnp.log(1.0 + jnp.exp(-jnp.abs(x)))


def _split2(x):
    h = x.astype(BF16)
    return h, (x - h.astype(F32)).astype(BF16)


def _split3(x):
    h = x.astype(BF16)
    r = x - h.astype(F32)
    m = r.astype(BF16)
    return h, m, (r - m.astype(F32)).astype(BF16)


def _dot3_parts(a_parts, b_parts):
    ah, al = a_parts
    bh, bl = b_parts
    return _dot(jnp.concatenate([ah, ah, al], axis=1), jnp.concatenate([bh, bl, bh], axis=0))


def _mod_kernel(c_ref, w_ref, b_ref, o_ref):
    c = c_ref[...]
    s = c * _sigmoid(c)
    o_ref[...] = _dot(s.astype(BF16), w_ref[...].astype(BF16)) + b_ref[...]


def _mod_call(c_pad, w_mod, b_mod):
    rows, d = c_pad.shape
    n = w_mod.shape[1]
    tn = 1024
    return pl.pallas_call(
        _mod_kernel,
        grid=(n // tn,),
        in_specs=[pl.BlockSpec((rows, d), lambda j: (0, 0)),
                  pl.BlockSpec((d, tn), lambda j: (0, j)),
                  pl.BlockSpec((1, tn), lambda j: (0, j))],
        out_specs=pl.BlockSpec((rows, tn), lambda j: (0, j)),
        out_shape=jax.ShapeDtypeStruct((rows, n), F32),
        compiler_params=_cparams(("arbitrary",)),
    )(c_pad, w_mod, b_mod.reshape(1, n))


def _inproj_kernel(x_ref, mod_ref, g_ref, w_ref, o_ref, ba_ref, h_scr, *, tn):
    j = pl.program_id(2)

    @pl.when(j == 0)
    def _():
        x = x_ref[0]
        ms = jnp.mean(x * x, axis=-1, keepdims=True)
        y = x * lax.rsqrt(ms + EPS) * g_ref[...]
        h = y * (1.0 + mod_ref[0, 1:2, :]) + mod_ref[0, 0:1, :]
        h_scr[...] = h.astype(BF16)

    acc = _dot(h_scr[...], w_ref[...])
    o_ref[0] = acc.astype(BF16)

    @pl.when(j == COL_BA // tn)
    def _():
        ba_ref[0] = acc[:, COL_BA % tn:COL_BA % tn + LANES]


def _inproj_call(x, mod3, g_mix, w_in_p):
    b, s, d = x.shape
    n = w_in_p.shape[1]
    tm = min(s, 1024)
    tn = n // 4
    assert n % 4 == 0 and tn % LANES == 0 and COL_BA % tn + LANES <= tn
    return pl.pallas_call(
        functools.partial(_inproj_kernel, tn=tn),
        grid=(b, s // tm, n // tn),
        in_specs=[pl.BlockSpec((1, tm, d), lambda bi, i, j: (bi, i, 0)),
                  pl.BlockSpec((1, 6, d), lambda bi, i, j: (bi, 0, 0)),
                  pl.BlockSpec((1, d), lambda bi, i, j: (0, 0)),
                  pl.BlockSpec((d, tn), lambda bi, i, j: (0, j))],
        out_specs=[pl.BlockSpec((1, tm, tn), lambda bi, i, j: (bi, i, j)),
                   pl.BlockSpec((1, tm, LANES), lambda bi, i, j: (bi, i, 0))],
        out_shape=[jax.ShapeDtypeStruct((b, s, n), BF16),
                   jax.ShapeDtypeStruct((b, s, LANES), F32)],
        scratch_shapes=[pltpu.VMEM((tm, d), BF16)],
        compiler_params=_cparams(("parallel", "parallel", "arbitrary")),
    )(x, mod3, g_mix.reshape(1, d), w_in_p)


def _dn_kernel(q_ref, k_ref, v_ref, z_ref, ba_ref, cwq_ref, cwk_ref, cwv_ref, alog_ref, dt_ref, og_ref,
               y_ref,
               xp, qn, kn, vn, aq_s, b_s, egl_s, o_s, *, seq, tr, ppb):
    head = pl.program_id(1)
    n_pairs = seq // PAIR
    n_tiles = seq // tr

    def conv_phase(x_ref, cw_ref, dst, normalise, scale):
        xp[0:8, :] = jnp.zeros((8, LANES), F32)
        xp[seq + 8:seq + 16, :] = jnp.zeros((8, LANES), F32)

        def copy_tile(t, c):
            r0 = pl.multiple_of(t * tr, tr)
            xp[pl.ds(r0 + 8, tr), :] = x_ref[0, pl.ds(r0, tr), :].astype(F32)
            return c

        lax.fori_loop(0, n_tiles, copy_tile, 0)
        cw = cw_ref[...]

        def tile(t, c):
            r0 = pl.multiple_of(t * tr, tr)
            acc = xp[pl.ds(r0 + 6, tr), :] * cw[0:1]
            for kk in range(1, CONV_K):
                acc = acc + xp[pl.ds(r0 + 6 + kk, tr), :] * cw[kk:kk + 1]
            y = acc * _sigmoid(acc)
            if normalise:
                y = y * lax.rsqrt(jnp.sum(y * y, axis=-1, keepdims=True) + EPS)
            if scale != 1.0:
                y = y * scale
            dst[pl.ds(r0, tr), :] = y
            return c

        lax.fori_loop(0, n_tiles, tile, 0)

    conv_phase(q_ref, cwq_ref, qn, True, DN_HEAD_DIM ** -0.5)
    conv_phase(k_ref, cwk_ref, kn, True, 1.0)
    conv_phase(v_ref, cwv_ref, vn, False, 1.0)

    ri = lax.broadcasted_iota(I32, (PAIR, PAIR), 0)
    ci = lax.broadcasted_iota(I32, (PAIR, PAIR), 1)
    same = (ri // CHUNK) == (ci // CHUNK)
    mask_incl = (same & (ci <= ri), same & (ci >= ri))
    mask_strict = (same & (ci < ri), same & (ci > ri))
    eye = jnp.where(ri == ci, 1.0, 0.0).astype(F32)
    first_chunk = ci < CHUNK
    lu = jnp.concatenate([jnp.where(mask_incl[0], 1.0, 0.0), jnp.where(mask_incl[1], 1.0, 0.0)],
                         axis=0).astype(BF16)
    lu3 = jnp.concatenate([lu, lu, lu], axis=1)
    alog_row = alog_ref[...]
    dt_row = dt_ref[...]

    def lane_bcast(x, col):
        shifted = pltpu.roll(x, shift=lax.rem(LANES - col, LANES), axis=1)
        return jnp.broadcast_to(shifted[:, 0:1], x.shape)

    def wide(x):
        return jnp.concatenate([x, x], axis=1)

    same16 = wide((ri // 16) == (ci // 16))
    same32 = wide((ri // 32) == (ci // 32))
    off16 = same32 & jnp.logical_not(same16)
    eye_w = wide(eye)
    zero_blk = jnp.zeros((PAIR, PAIR), BF16)

    def bdiag(w):
        return jnp.concatenate([jnp.concatenate([w[:, :PAIR], zero_blk], axis=1),
                                jnp.concatenate([zero_blk, w[:, PAIR:]], axis=1)], axis=0)

    def tri_inverse_many(lws):
        d0s = [jnp.where(same16, lw, 0.0) for lw in lws]
        ms = [(-d0).astype(BF16) for d0 in d0s]
        xs = [eye_w - d0 for d0 in d0s]
        for _ in range(3):
            ms = [_dot(m, bdiag(m)).astype(BF16) for m in ms]
            xs = [x + _dot(x.astype(BF16), bdiag(m)) for x, m in zip(xs, ms)]
        for level in range(2):
            cs = [bdiag((jnp.where(off16, lw, 0.0) if level == 0 else jnp.where(same32, 0.0, lw)).astype(BF16))
                  for lw in lws]
            xbs = [x.astype(BF16) for x in xs]
            ts = [_dot(xb, c).astype(BF16) for xb, c in zip(xbs, cs)]
            xs = [x - _dot(t, bdiag(xb)) for x, t, xb in zip(xs, ts, xbs)]
        rs = []
        for lw, x in zip(lws, xs):
            lh, ll = _split2(lw)
            xh, xl = _split2(x)
            prod = _dot(jnp.concatenate([lh, lh, ll], axis=1),
                        jnp.concatenate([bdiag(xh), bdiag(xl), bdiag(xh)], axis=0))
            rs.append((eye_w - x) - prod)
        return [x + _dot(x.astype(BF16), bdiag(r.astype(BF16))) for x, r in zip(xs, rs)]

    def gate_stage(pairs):
        r0s = [pl.multiple_of(p * PAIR, PAIR) for p in pairs]
        blks = [ba_ref[0, pl.ds(r0, PAIR), :] for r0 in r0s]
        gs = [-jnp.exp(alog_row) * _softplus(blk + dt_row) for blk in blks]
        betas = [jnp.concatenate([lane_bcast(sg, head), lane_bcast(sg, DN_HEADS + head)], axis=1)
                 for sg in [_sigmoid(blk) for blk in blks]]
        g_reps = [jnp.concatenate([lane_bcast(g, 2 * DN_HEADS + head), lane_bcast(g, 3 * DN_HEADS + head)],
                                  axis=1) for g in gs]
        gps = [jnp.concatenate(_split3(g_rep), axis=0) for g_rep in g_reps]
        kqs = []
        for r0 in r0s:
            kb = kn[pl.ds(r0, PAIR), :].astype(BF16)
            kqs.append(_dot_nt(jnp.concatenate([kb, qn[pl.ds(r0, PAIR), :].astype(BF16)], axis=0), kb))
        css = [_dot(lu3, gp) for gp in gps]
        out = []
        for r0, beta, g_rep, cs, kq in zip(r0s, betas, g_reps, css, kqs):
            pre, suf = cs[:PAIR], cs[PAIR:]
            per_dir = ((beta[:, :LANES], pre[:, :LANES], suf[:, :LANES] - g_rep[:, :LANES]),
                       (beta[:, LANES:], suf[:, LANES:], pre[:, LANES:] - g_rep[:, LANES:]))
            out.append((r0, per_dir, kq[:PAIR], kq[PAIR:]))
        return out

    def finish_stage(chains, t_invs):
        egs, wus = [], []
        for (p, r0, d, beta, gc, ex, qk, dec), t_inv in zip(chains, t_invs):
            k2 = kn[pl.ds(r0, PAIR), :]
            eg = jnp.exp(gc)
            vb = (vn[pl.ds(r0, PAIR), :] * beta).astype(BF16)
            kbg = (k2 * beta * eg).astype(BF16)
            egs.append(eg)
            wus.append(_dot(t_inv.astype(BF16), jnp.concatenate([kbg, vb], axis=1)).astype(BF16))
        abs_, qos = [], []
        for (p, r0, d, beta, gc, ex, qk, dec), wu in zip(chains, wus):
            kst = (kn[pl.ds(r0, PAIR), :] * jnp.exp(ex)).T.astype(BF16)
            kst2 = jnp.concatenate([jnp.where(first_chunk, kst, 0.0), jnp.where(first_chunk, 0.0, kst)],
                                   axis=0).astype(BF16)
            abs_.append(_dot(kst2, wu))
            intra = jnp.where(mask_incl[d], qk * dec, 0.0).astype(BF16)
            qos.append(_dot(intra, wu))
        parts, comps = [], []
        for (p, r0, d, beta, gc, ex, qk, dec), eg, ab, qo in zip(chains, egs, abs_, qos):
            c1, c2 = (0, 1) if d == 0 else (1, 0)
            a = (-ab[:PAIR, :LANES], -ab[PAIR:, :LANES])
            b = (ab[:PAIR, LANES:], ab[PAIR:, LANES:])
            qp = qn[pl.ds(r0, PAIR), :] * eg - qo[:, :LANES]
            q = (qp[:CHUNK], qp[CHUNK:])
            comps.append(_dot(jnp.concatenate([a[c2], q[c2]], axis=0).astype(BF16),
                              jnp.concatenate([a[c1], b[c1]], axis=1).astype(BF16)))
            parts.append((a, b, q, c1, c2))
        for (p, r0, d, beta, gc, ex, qk, dec), qo, (a, b, q, c1, c2), comp in zip(chains, qos, parts, comps):
            etot = jnp.exp(gc + ex)
            e1, e2 = etot[c1 * CHUNK:c1 * CHUNK + 1], etot[c2 * CHUNK:c2 * CHUNK + 1]
            m = e2 * a[c1] + e1 * a[c2] + comp[:PAIR, :LANES]
            a0 = pl.multiple_of(p * AQ_ROWS, AQ_ROWS)
            aq_s[d, pl.ds(a0, PAIR), :] = m.astype(BF16)
            aq_s[d, pl.ds(a0 + PAIR, CHUNK), :] = q[c1].astype(BF16)
            aq_s[d, pl.ds(a0 + PAIR + CHUNK, CHUNK), :] = (e1 * q[c2] + comp[PAIR:, :LANES]).astype(BF16)
            b_s[d, pl.ds(r0, PAIR), :] = e2 * b[c1] + comp[:PAIR, LANES:] + b[c2]
            o1 = pl.ds(r0 + c1 * CHUNK, CHUNK)
            o2 = pl.ds(r0 + c2 * CHUNK, CHUNK)
            o_s[o1, :] = o_s[o1, :] + qo[c1 * CHUNK:(c1 + 1) * CHUNK, LANES:]
            o_s[o2, :] = o_s[o2, :] + qo[c2 * CHUNK:(c2 + 1) * CHUNK, LANES:] + comp[PAIR:, LANES:]
            e0 = pl.multiple_of(p * 8, 8)
            egl_s[d, pl.ds(e0, 8), :] = etot[0:8] * etot[CHUNK:CHUNK + 8]

    def prep_block(i, c):
        pairs = [i * ppb + j for j in range(ppb)]
        chains, lws = [], []
        for p, (r0, per_dir, kk, qk) in zip(pairs, gate_stage(pairs)):
            lms = []
            for d in range(2):
                beta, gc, ex = per_dir[d]
                m_in = mask_incl[d]
                dec = jnp.where(m_in, jnp.exp(jnp.where(m_in, gc - gc.T, 0.0)), 0.0)
                lms.append(jnp.where(mask_strict[d], beta * kk * dec, 0.0))
                chains.append((p, r0, d, beta, gc, ex, qk, dec))
            lws.append(jnp.concatenate(lms, axis=1))
        t_invs = []
        for t_w in tri_inverse_many(lws):
            t_invs += [t_w[:, :PAIR], t_w[:, PAIR:]]
        finish_stage(chains, t_invs)
        return c

    def zero_tile(t, c):
        r0 = pl.multiple_of(t * tr, tr)
        o_s[pl.ds(r0, tr), :] = jnp.zeros((tr, LANES), F32)
        return c

    lax.fori_loop(0, n_tiles, zero_tile, 0)
    lax.fori_loop(0, n_pairs // ppb, prep_block, 0)

    def pair_step(d, pair, state):
        c1 = d
        r0 = pl.multiple_of(pair * PAIR, PAIR)
        a0 = pl.multiple_of(pair * AQ_ROWS, AQ_ROWS)
        res = _dot(aq_s[d, pl.ds(a0, AQ_ROWS), :], state.astype(BF16))
        o1 = pl.ds(r0 + c1 * CHUNK, CHUNK)
        o2 = pl.ds(r0 + (1 - c1) * CHUNK, CHUNK)
        o_s[o1, :] = o_s[o1, :] + res[PAIR:PAIR + CHUNK]
        o_s[o2, :] = o_s[o2, :] + res[PAIR + CHUNK:]
        eg = egl_s[d, pl.ds(pl.multiple_of(pair * 8, 8), 8), :][0:1, :]
        return state * eg + res[:PAIR] + b_s[d, pl.ds(r0, PAIR), :]

    def pair_scan(n, carry):
        s_f, s_b = carry
        return pair_step(0, n, s_f), pair_step(1, n_pairs - 1 - n, s_b)

    zero_state = jnp.zeros((DN_HEAD_DIM, DN_HEAD_DIM), F32)
    lax.fori_loop(0, n_pairs, pair_scan, (zero_state, zero_state))

    og = og_ref[...]

    def out_tile(t, c):
        r0 = pl.multiple_of(t * tr, tr)
        o = o_s[pl.ds(r0, tr), :]
        z = z_ref[0, pl.ds(r0, tr), :].astype(F32)
        y = o * lax.rsqrt(jnp.mean(o * o, axis=-1, keepdims=True) + EPS) * og
        y_ref[0, pl.ds(r0, tr), :] = (y * (z * _sigmoid(z))).astype(BF16)
        return c

    lax.fori_loop(0, n_tiles, out_tile, 0)


def _dn_call(proj, ba, conv_w, alog_row, dt_row, o_gain):
    b, s, _ = proj.shape
    hd = DN_HEAD_DIM
    tr = min(s, 1024)
    qb, kb, vb, zb = COL_Q // hd, COL_K // hd, COL_V // hd, COL_Z // hd

    def col(base):
        return pl.BlockSpec((1, s, hd), lambda bi, h: (bi, 0, base + h))

    def cw(base):
        return pl.BlockSpec((CONV_K, hd), lambda bi, h: (0, base + h))

    row = pl.BlockSpec((1, LANES), lambda bi, h: (0, 0))
    n_pairs = s // PAIR
    ppb = next(n for n in (8, 4, 2, 1) if n_pairs % n == 0)
    kern = functools.partial(_dn_kernel, seq=s, tr=tr, ppb=ppb)
    return pl.pallas_call(
        kern,
        grid=(b, DN_HEADS),
        in_specs=[col(qb), col(kb), col(vb), col(zb),
                  pl.BlockSpec((1, s, LANES), lambda bi, h: (bi, 0, 0)),
                  cw(qb), cw(kb), cw(vb), row, row, row],
        out_specs=pl.BlockSpec((1, s, hd), lambda bi, h: (bi, 0, h)),
        out_shape=jax.ShapeDtypeStruct((b, s, DN_HEADS * hd), BF16),
        scratch_shapes=[pltpu.VMEM((s + 16, LANES), F32),
                        pltpu.VMEM((s, hd), F32), pltpu.VMEM((s, hd), F32), pltpu.VMEM((s, hd), F32),
                        pltpu.VMEM((2, s // PAIR * AQ_ROWS, hd), BF16),
                        pltpu.VMEM((2, s // PAIR * hd, hd), F32),
                        pltpu.VMEM((2, s // PAIR * 8, LANES), F32),
                        pltpu.VMEM((s, hd), F32)],
        compiler_params=_cparams(("parallel", "parallel")),
    )(proj, proj, proj, proj, ba, conv_w, conv_w, conv_w, alog_row, dt_row, o_gain.reshape(1, hd))


def _mlaproj_kernel(cq_ref, ckv_ref, kr_ref, pos_ref, invf_ref, sgn_ref, qg_ref, kvg_ref,
                    wqn_ref, wqr_ref, wqs_ref, wkn_ref, wv_ref,
                    q_ref, k_ref, v_ref):
    cq = cq_ref[0].astype(F32)
    hq = (cq * lax.rsqrt(jnp.mean(cq * cq, axis=-1, keepdims=True) + EPS) * qg_ref[...]).astype(BF16)
    qn = _dot(hq, wqn_ref[...])
    qr = _dot(hq, wqr_ref[...])
    qs = _dot(hq, wqs_ref[...])
    ang = pos_ref[0].astype(F32) * invf_ref[...]
    cc = jnp.cos(ang)
    ss = jnp.sin(ang) * sgn_ref[...]
    scale = QK_DIM ** -0.5 * math.log2(math.e)
    ckv = ckv_ref[0].astype(F32)
    hkv = (ckv * lax.rsqrt(jnp.mean(ckv * ckv, axis=-1, keepdims=True) + EPS) * kvg_ref[...]).astype(BF16)
    kn = _dot(hkv, wkn_ref[...])
    vv = _dot(hkv, wv_ref[...])
    kr = kr_ref[0].astype(F32)
    k_rope = (kr[:, :ROPE_DIM] * cc + kr[:, ROPE_DIM:] * ss).astype(BF16)
    for h in range(MLA_HEADS):
        lo, hi = h * ROPE_DIM, (h + 1) * ROPE_DIM
        q_rope = qr[:, lo:hi] * cc + qs[:, lo:hi] * ss
        q_ref[0, h, :, 0:NOPE_DIM] = (qn[:, h * NOPE_DIM:(h + 1) * NOPE_DIM] * scale).astype(BF16)
        q_ref[0, h, :, NOPE_DIM:QK_DIM] = (q_rope * scale).astype(BF16)
        k_ref[0, h, :, 0:NOPE_DIM] = kn[:, h * NOPE_DIM:(h + 1) * NOPE_DIM].astype(BF16)
        k_ref[0, h, :, NOPE_DIM:QK_DIM] = k_rope
        v_ref[0, h] = vv[:, h * V_DIM:(h + 1) * V_DIM].astype(BF16)


def _mlaproj_call(proj, pos3, invf, sgn, q_gain, kv_gain, wqn, wqr, wqs, wkn, wv):
    b, s, _ = proj.shape
    tm = min(s, 512)
    nh = MLA_HEADS

    def full(a):
        return pl.BlockSpec(a.shape, lambda bi, i: (0,) * a.ndim)

    qg = q_gain.reshape(1, Q_LORA)
    kvg = kv_gain.reshape(1, KV_LORA)
    return pl.pallas_call(
        _mlaproj_kernel,
        grid=(b, s // tm),
        in_specs=[pl.BlockSpec((1, tm, Q_LORA), lambda bi, i: (bi, i, COL_CQ // Q_LORA)),
                  pl.BlockSpec((1, tm, KV_LORA), lambda bi, i: (bi, i, COL_CKV // KV_LORA)),
                  pl.BlockSpec((1, tm, LANES), lambda bi, i: (bi, i, COL_KR // LANES)),
                  pl.BlockSpec((1, tm, 1), lambda bi, i: (bi, i, 0)),
                  full(invf), full(sgn), full(qg), full(kvg),
                  full(wqn), full(wqr), full(wqs), full(wkn), full(wv)],
        out_specs=[pl.BlockSpec((1, nh, tm, QK_DIM), lambda bi, i: (bi, 0, i, 0)),
                   pl.BlockSpec((1, nh, tm, QK_DIM), lambda bi, i: (bi, 0, i, 0)),
                   pl.BlockSpec((1, nh, tm, V_DIM), lambda bi, i: (bi, 0, i, 0))],
        out_shape=[jax.ShapeDtypeStruct((b, nh, s, QK_DIM), BF16),
                   jax.ShapeDtypeStruct((b, nh, s, QK_DIM), BF16),
                   jax.ShapeDtypeStruct((b, nh, s, V_DIM), BF16)],
        compiler_params=_cparams(("parallel", "parallel")),
    )(proj, proj, proj, pos3, invf, sgn, qg, kvg, wqn, wqr, wqs, wkn, wv)


def _attn_kernel(q_ref, k_ref, v_ref, o_ref, *, seq, tq, sub, tk):
    n_sub, n_c = tq // sub, seq // tk

    def scores(a, c):
        return _dot_nt(q_ref[0, 0, a * sub:(a + 1) * sub, :], k_ref[0, 0, c * tk:(c + 1) * tk, :])

    s_cur = [scores(0, c) for c in range(n_c)]
    for a in range(n_sub):
        mx = s_cur[0]
        for c in range(1, n_c):
            mx = jnp.maximum(mx, s_cur[c])
        m = jnp.max(mx, axis=-1, keepdims=True)
        lp = jnp.zeros((sub, LANES), F32)
        acc = jnp.zeros((sub, V_DIM), F32)
        s_next = []
        for c in range(n_c):
            p = jnp.exp2(s_cur[c] - m)
            for j in range(tk // LANES):
                lp = lp + p[:, j * LANES:(j + 1) * LANES]
            acc = acc + _dot(p.astype(BF16), v_ref[0, 0, c * tk:(c + 1) * tk, :])
            if a + 1 < n_sub:
                s_next.append(scores(a + 1, c))
        l = jnp.sum(lp, axis=-1, keepdims=True)
        o_ref[0, a * sub:(a + 1) * sub, :] = (acc / l).astype(BF16)
        s_cur = s_next


def _attn_call(q, k, v):
    b, nh, s, _ = q.shape
    tq = min(s, 2048)
    kern = functools.partial(_attn_kernel, seq=s, tq=tq, sub=min(tq, 256), tk=min(s, 512))
    return pl.pallas_call(
        kern,
        grid=(b, nh, s // tq),
        in_specs=[pl.BlockSpec((1, 1, tq, QK_DIM), lambda bi, h, i: (bi, h, i, 0)),
                  pl.BlockSpec((1, 1, s, QK_DIM), lambda bi, h, i: (bi, h, 0, 0)),
                  pl.BlockSpec((1, 1, s, V_DIM), lambda bi, h, i: (bi, h, 0, 0))],
        out_specs=pl.BlockSpec((1, tq, V_DIM), lambda bi, h, i: (bi, i, h)),
        out_shape=jax.ShapeDtypeStruct((b, s, nh * V_DIM), BF16),
        compiler_params=_cparams(("parallel", "parallel", "arbitrary")),
    )(q, k, v)


def _merge_kernel(ydn_ref, ymla_ref, gdn_ref, gmla_ref, x_ref, mod_ref, wodn_ref, womla_ref, wout_ref,
                  gffn_ref, wrt_ref,
                  x1_ref, h2_ref, aff_ref):
    y_dn = _dot(ydn_ref[0], wodn_ref[...])
    y_mla = _dot(ymla_ref[0], womla_ref[...])
    merged = _sigmoid(gdn_ref[0].astype(F32)) * y_dn + _sigmoid(gmla_ref[0].astype(F32)) * y_mla
    x1 = x_ref[0] + mod_ref[0, 2:3, :] * _dot(merged.astype(BF16), wout_ref[...])
    x1_ref[0] = x1
    y = x1 * lax.rsqrt(jnp.mean(x1 * x1, axis=-1, keepdims=True) + EPS) * gffn_ref[...]
    h2 = (y * (1.0 + mod_ref[0, 4:5, :]) + mod_ref[0, 3:4, :]).astype(BF16)
    h2_ref[0] = h2
    logits = _dot_nt(wrt_ref[...], h2)
    mx = jnp.max(logits, axis=0, keepdims=True)
    ex = jnp.exp(logits - mx)
    aff_ref[0] = ex / jnp.sum(ex, axis=0, keepdims=True)


def _merge_call(y_dn, y_mla, proj, x, mod3, w_o_dn, w_o_mla, w_out, g_ffn, w_router_t):
    b, s, d = x.shape
    tm = min(s, 512)
    ne = w_router_t.shape[0]

    def tok(width):
        return pl.BlockSpec((1, tm, width), lambda bi, i: (bi, i, 0))

    def full(a):
        return pl.BlockSpec(a.shape, lambda bi, i: (0,) * a.ndim)

    gf = g_ffn.reshape(1, d)
    return pl.pallas_call(
        _merge_kernel,
        grid=(b, s // tm),
        in_specs=[tok(d), tok(d),
                  pl.BlockSpec((1, tm, d), lambda bi, i: (bi, i, COL_GATE // d)),
                  pl.BlockSpec((1, tm, d), lambda bi, i: (bi, i, COL_GATE // d + 1)),
                  tok(d),
                  pl.BlockSpec((1, 6, d), lambda bi, i: (bi, 0, 0)),
                  full(w_o_dn), full(w_o_mla), full(w_out), full(gf), full(w_router_t)],
        out_specs=[tok(d), tok(d), pl.BlockSpec((1, ne, tm), lambda bi, i: (bi, 0, i))],
        out_shape=[jax.ShapeDtypeStruct((b, s, d), F32),
                   jax.ShapeDtypeStruct((b, s, d), BF16),
                   jax.ShapeDtypeStruct((b, ne, s), F32)],
        compiler_params=_cparams(("parallel", "parallel")),
    )(y_dn, y_mla, proj, proj, x, mod3, w_o_dn, w_o_mla, w_out, gf, w_router_t)


def _route_kernel(aff_ref, pos_ref, *, seq, cap):
    a = aff_ref[0]
    ne = a.shape[0]
    capf = jnp.float32(cap)

    def count_ge(t_bits):
        t = pltpu.bitcast(t_bits, F32)
        return jnp.sum(jnp.where(a >= t, 1.0, 0.0), axis=1, keepdims=True)

    def search(_, carry):
        lo, hi = carry
        mid = lo + ((hi - lo + 1) >> 1)
        ok = count_ge(mid) >= capf
        return jnp.where(ok, mid, lo), jnp.where(ok, hi, mid - 1)

    lo0 = jnp.zeros((ne, 1), I32)
    hi0 = jnp.full((ne, 1), 0x7F800000, I32)
    thr_bits, _ = lax.fori_loop(0, 32, search, (lo0, hi0))
    thr = pltpu.bitcast(thr_bits, F32)

    ur = lax.broadcasted_iota(I32, (LANES, LANES), 0)
    uc = lax.broadcasted_iota(I32, (LANES, LANES), 1)
    upper = jnp.where(ur <= uc, 1.0, 0.0).astype(BF16)

    def prefix_incl(mask):
        carry = jnp.zeros((ne, 1), F32)
        outs = []
        for j in range(seq // LANES):
            pr = _dot(mask[:, j * LANES:(j + 1) * LANES].astype(BF16), upper) + carry
            outs.append(pr)
            carry = pr[:, LANES - 1:LANES]
        return jnp.concatenate(outs, axis=1)

    gt = a > thr
    eq = a == thr
    n_gt = jnp.sum(jnp.where(gt, 1.0, 0.0), axis=1, keepdims=True)
    take_eq = eq & (prefix_incl(jnp.where(eq, 1.0, 0.0)) <= capf - n_gt)
    sel = gt | take_eq
    slot = prefix_incl(jnp.where(sel, 1.0, 0.0)) - 1.0
    pos_ref[0] = jnp.where(sel, slot, -1.0).astype(I32)


def _route_call(aff, cap):
    b, ne, s = aff.shape
    kern = functools.partial(_route_kernel, seq=s, cap=cap)
    return pl.pallas_call(
        kern,
        grid=(b,),
        in_specs=[pl.BlockSpec((1, ne, s), lambda bi: (bi, 0, 0))],
        out_specs=pl.BlockSpec((1, ne, s), lambda bi: (bi, 0, 0)),
        out_shape=jax.ShapeDtypeStruct((b, ne, s), I32),
        compiler_params=_cparams(("parallel",)),
    )(aff)


def _gather_kernel(pos_ref, aff_ref, h_ref, xe_ref, gate_ref, *, seq, cap, tk):
    slot = lax.broadcasted_iota(I32, (cap, 1), 0)
    acc = jnp.zeros((cap, h_ref.shape[-1]), F32)
    gacc = jnp.zeros((cap, 1), F32)
    for kt in range(seq // tk):
        pm = pos_ref[0, 0, :, kt * tk:(kt + 1) * tk]
        hit = pm == slot
        acc = acc + _dot(jnp.where(hit, 1.0, 0.0).astype(BF16), h_ref[0, kt * tk:(kt + 1) * tk, :])
        am = aff_ref[0, 0, :, kt * tk:(kt + 1) * tk]
        gacc = gacc + jnp.sum(jnp.where(hit, am, 0.0), axis=1, keepdims=True)
    xe_ref[0, 0] = acc.astype(BF16)
    gate_ref[0, 0] = jnp.broadcast_to(gacc, (cap, LANES))


def _gather_call(pos4, aff4, h2, cap):
    b, ne, _, s = pos4.shape
    d = h2.shape[-1]
    tk = min(s, 512)
    kern = functools.partial(_gather_kernel, seq=s, cap=cap, tk=tk)
    return pl.pallas_call(
        kern,
        grid=(b, ne),
        in_specs=[pl.BlockSpec((1, 1, 1, s), lambda bi, e: (bi, e, 0, 0)),
                  pl.BlockSpec((1, 1, 1, s), lambda bi, e: (bi, e, 0, 0)),
                  pl.BlockSpec((1, s, d), lambda bi, e: (bi, 0, 0))],
        out_specs=[pl.BlockSpec((1, 1, cap, d), lambda bi, e: (bi, e, 0, 0)),
                   pl.BlockSpec((1, 1, cap, LANES), lambda bi, e: (bi, e, 0, 0))],
        out_shape=[jax.ShapeDtypeStruct((b, ne, cap, d), BF16),
                   jax.ShapeDtypeStruct((b, ne, cap, LANES), F32)],
        compiler_params=_cparams(("parallel", "arbitrary")),
    )(pos4, aff4, h2)


def _ffn_kernel(xe_ref, gate_ref, wg_ref, wu_ref, wd_ref, ye_ref):
    x = xe_ref[0, 0]
    g = _dot(x, wg_ref[0].astype(BF16))
    u = _dot(x, wu_ref[0].astype(BF16))
    hid = (g * _sigmoid(g)) * u
    y = _dot(hid.astype(BF16), wd_ref[0].astype(BF16))
    ye_ref[0, 0] = (y * gate_ref[0, 0][:, 0:1]).astype(BF16)


def _ffn_call(xe, gate, wg, wu, wd):
    b, ne, cap, d = xe.shape
    f = wg.shape[-1]
    return pl.pallas_call(
        _ffn_kernel,
        grid=(ne, b),
        in_specs=[pl.BlockSpec((1, 1, cap, d), lambda e, bi: (bi, e, 0, 0)),
                  pl.BlockSpec((1, 1, cap, LANES), lambda e, bi: (bi, e, 0, 0)),
                  pl.BlockSpec((1, d, f), lambda e, bi: (e, 0, 0)),
                  pl.BlockSpec((1, d, f), lambda e, bi: (e, 0, 0)),
                  pl.BlockSpec((1, f, d), lambda e, bi: (e, 0, 0))],
        out_specs=pl.BlockSpec((1, 1, cap, d), lambda e, bi: (bi, e, 0, 0)),
        out_shape=jax.ShapeDtypeStruct((b, ne, cap, d), BF16),
        compiler_params=_cparams(("parallel", "arbitrary")),
    )(xe, gate, wg, wu, wd)


def _combine_kernel(pos_ref, ye_ref, x1_ref, mod_ref, gfin_ref, o_ref, acc_ref, *, cap, last_norm):
    g = pl.program_id(2)
    eg = ye_ref.shape[1]

    @pl.when(g == 0)
    def _():
        acc_ref[...] = jnp.zeros_like(acc_ref)

    slot = lax.broadcasted_iota(I32, (1, cap), 1)
    pos = pos_ref[0, 0]
    onehot = jnp.concatenate(
        [jnp.where(pos[:, j:j + 1] == slot, 1.0, 0.0).astype(BF16) for j in range(eg)], axis=1)
    acc_ref[...] += _dot(onehot, ye_ref[0].reshape(eg * cap, ye_ref.shape[-1]))

    @pl.when(g == pl.num_programs(2) - 1)
    def _():
        x2 = x1_ref[0] + mod_ref[0, 5:6, :] * acc_ref[...]
        if last_norm:
            x2 = x2 * lax.rsqrt(jnp.mean(x2 * x2, axis=-1, keepdims=True) + EPS) * gfin_ref[...]
        o_ref[0] = x2


def _combine_call(pos_tok, ye, x1, mod3, g_final, cap, last_norm):
    b, s, d = x1.shape
    ne = ye.shape[1]
    n_groups, eg = pos_tok.shape[1], pos_tok.shape[3]
    tm = min(s, 512)
    kern = functools.partial(_combine_kernel, cap=cap, last_norm=last_norm)
    return pl.pallas_call(
        kern,
        grid=(b, s // tm, n_groups),
        in_specs=[pl.BlockSpec((1, 1, tm, eg), lambda bi, i, e: (bi, e, i, 0)),
                  pl.BlockSpec((1, eg, cap, d), lambda bi, i, e: (bi, e, 0, 0)),
                  pl.BlockSpec((1, tm, d), lambda bi, i, e: (bi, i, 0)),
                  pl.BlockSpec((1, 6, d), lambda bi, i, e: (bi, 0, 0)),
                  pl.BlockSpec((1, d), lambda bi, i, e: (0, 0))],
        out_specs=pl.BlockSpec((1, tm, d), lambda bi, i, e: (bi, i, 0)),
        out_shape=jax.ShapeDtypeStruct((b, s, d), F32),
        scratch_shapes=[pltpu.VMEM((tm, d), F32)],
        compiler_params=_cparams(("parallel", "parallel", "arbitrary")),
    )(pos_tok, ye, x1, mod3, g_final.reshape(1, d))


def _pack_w_in(w):
    o_ba = 4 * 1024 + 0
    o_b = o_ba
    o_a = o_b + 2 * DN_HEADS
    o_cq = o_a + 2 * DN_HEADS
    o_ckv = o_cq + Q_LORA
    o_kr = o_ckv + KV_LORA
    o_g = o_kr + ROPE_DIM
    half = ROPE_DIM // 2
    w_kr = w[:, o_kr:o_g]
    w_kr_sw = jnp.concatenate([w_kr[:, half:], w_kr[:, :half]], axis=1)
    pad = jnp.zeros((w.shape[0], COL_GATE - COL_BA - 4 * DN_HEADS), w.dtype)
    packed = jnp.concatenate([w[:, :o_ba], w[:, o_cq:o_ckv], w[:, o_ckv:o_kr], w_kr, w_kr_sw,
                              w[:, o_b:o_a], w[:, o_a:o_cq], pad, w[:, o_g:]], axis=1)
    assert packed.shape[1] == N_PAD
    return packed.astype(BF16)


def _lane_row(vals, offset):
    row = jnp.zeros((1, LANES), F32)
    return row.at[0, offset:offset + vals.size].set(vals.reshape(-1).astype(F32))


def kernel(x, c, positions, w_mod, b_mod, g_mix, w_in, conv_w, a_log, dt_bias, dn_o_gain, q_gain, w_uq,
           kv_gain, w_ukv, w_o_dn, w_o_mla, w_out, g_ffn, w_router, w_gate, w_up, w_down, g_final):
    b, s, d = x.shape
    depth = w_mod.shape[0]
    cap = CAPACITY_FACTOR * s // N_EXPERTS
    half = ROPE_DIM // 2
    inv_freq = ROPE_THETA ** (-jnp.arange(half, dtype=F32) / half)
    invf = jnp.concatenate([inv_freq, inv_freq]).reshape(1, ROPE_DIM)
    sgn = jnp.concatenate([-jnp.ones((half,), F32), jnp.ones((half,), F32)]).reshape(1, ROPE_DIM)
    pos3 = positions.reshape(b, s, 1)
    c_pad = jnp.zeros((8, d), F32).at[:b].set(c)
    swap = np.concatenate([np.arange(half, ROPE_DIM), np.arange(half)])

    for l in range(depth):
        mod3 = _mod_call(c_pad, w_mod[l], b_mod[l])[:b].reshape(b, 6, d)
        proj, ba = _inproj_call(x, mod3, g_mix[l], _pack_w_in(w_in[l]))

        alog_row = _lane_row(a_log[l], 2 * DN_HEADS)
        dt_row = _lane_row(dt_bias[l], 2 * DN_HEADS)
        y_dn = _dn_call(proj, ba, conv_w[l], alog_row, dt_row, dn_o_gain[l])

        wq = w_uq[l].reshape(Q_LORA, MLA_HEADS, QK_DIM)
        wqn = wq[:, :, :NOPE_DIM].reshape(Q_LORA, MLA_HEADS * NOPE_DIM).astype(BF16)
        wqr = wq[:, :, NOPE_DIM:].reshape(Q_LORA, MLA_HEADS * ROPE_DIM).astype(BF16)
        wqs = wq[:, :, NOPE_DIM:][:, :, swap].reshape(Q_LORA, MLA_HEADS * ROPE_DIM).astype(BF16)
        wkv = w_ukv[l].reshape(KV_LORA, MLA_HEADS, NOPE_DIM + V_DIM)
        wkn = wkv[:, :, :NOPE_DIM].reshape(KV_LORA, MLA_HEADS * NOPE_DIM).astype(BF16)
        wv = wkv[:, :, NOPE_DIM:].reshape(KV_LORA, MLA_HEADS * V_DIM).astype(BF16)
        q, k, v = _mlaproj_call(proj, pos3, invf, sgn, q_gain[l], kv_gain[l], wqn, wqr, wqs, wkn, wv)
        y_mla = _attn_call(q, k, v)

        x1, h2, aff = _merge_call(y_dn, y_mla, proj, x, mod3, w_o_dn[l].astype(BF16),
                                  w_o_mla[l].astype(BF16), w_out[l].astype(BF16), g_ffn[l],
                                  w_router[l].T.astype(BF16))
        pos = _route_call(aff, cap)
        pos4 = pos.reshape(b, N_EXPERTS, 1, s)
        aff4 = aff.reshape(b, N_EXPERTS, 1, s)
        xe, gate = _gather_call(pos4, aff4, h2, cap)
        ye = _ffn_call(xe, gate, w_gate[l], w_up[l], w_down[l])
        pos_tok = pos.reshape(b, N_EXPERTS // COMBINE_GROUP, COMBINE_GROUP, s).transpose(0, 1, 3, 2)
        x = _combine_call(pos_tok, ye, x1, mod3, g_final, cap, last_norm=(l == depth - 1))
    return x
```

```python
import functools
import math

import jax
import jax.numpy as jnp
import numpy as np
from jax import lax
from jax.experimental import pallas as pl
from jax.experimental.pallas import tpu as pltpu

F32 = jnp.float32
BF16 = jnp.bfloat16
I32 = jnp.int32

EPS = 1e-6
DN_HEADS = 8
DN_HEAD_DIM = 128
CONV_K = 5
CHUNK = 64
PAIR = 2 * CHUNK
AQ_ROWS = DN_HEAD_DIM + PAIR
MLA_HEADS = 8
NOPE_DIM = 128
ROPE_DIM = 64
V_DIM = 128
QK_DIM = NOPE_DIM + ROPE_DIM
Q_LORA = 512
KV_LORA = 256
ROPE_THETA = 10000.0
N_EXPERTS = 16
CAPACITY_FACTOR = 2
COMBINE_GROUP = 8
LANES = 128
VMEM_LIMIT = 56 * 1024 * 1024

COL_Q, COL_K, COL_V, COL_Z = 0, 1024, 2048, 3072
COL_CQ, COL_CKV, COL_KR, COL_BA, COL_GATE = 4096, 4608, 4864, 4992, 5120
N_PAD = 7168


def _cparams(sem):
    return pltpu.CompilerParams(dimension_semantics=sem, vmem_limit_bytes=VMEM_LIMIT)


def _dot(a, b):
    return jnp.dot(a, b, preferred_element_type=F32)


def _dot_nt(a, b):
    return lax.dot_general(a, b, (((1,), (1,)), ((), ())), preferred_element_type=F32)


def _dot_tn(a, b):
    return lax.dot_general(a, b, (((0,), (0,)), ((), ())), preferred_element_type=F32)


def _sigmoid(x):
    return 1.0 / (1.0 + jnp.exp(-x))


def _softplus(x):
    return jnp.maximum(x, 0.0) + jnp.log(1.0 + jnp.exp(-jnp.abs(x)))


def _split2(x):
    h = x.astype(BF16)
    return h, (x - h.astype(F32)).astype(BF16)


def _split3(x):
    h = x.astype(BF16)
    r = x - h.astype(F32)
    m = r.astype(BF16)
    return h, m, (r - m.astype(F32)).astype(BF16)


def _dot3_parts(a_parts, b_parts):
    ah, al = a_parts
    bh, bl = b_parts
    return _dot(jnp.concatenate([ah, ah, al], axis=1), jnp.concatenate([bh, bl, bh], axis=0))


def _mod_kernel(c_ref, w_ref, b_ref, o_ref):
    c = c_ref[...]
    s = c * _sigmoid(c)
    o_ref[...] = _dot(s.astype(BF16), w_ref[...].astype(BF16)) + b_ref[...]


def _mod_call(c_pad, w_mod, b_mod):
    rows, d = c_pad.shape
    n = w_mod.shape[1]
    tn = 1024
    return pl.pallas_call(
        _mod_kernel,
        grid=(n // tn,),
        in_specs=[pl.BlockSpec((rows, d), lambda j: (0, 0)),
                  pl.BlockSpec((d, tn), lambda j: (0, j)),
                  pl.BlockSpec((1, tn), lambda j: (0, j))],
        out_specs=pl.BlockSpec((rows, tn), lambda j: (0, j)),
        out_shape=jax.ShapeDtypeStruct((rows, n), F32),
        compiler_params=_cparams(("arbitrary",)),
    )(c_pad, w_mod, b_mod.reshape(1, n))


def _inproj_kernel(x_ref, mod_ref, g_ref, w_ref, o_ref, ba_ref, h_scr, *, tn):
    j = pl.program_id(2)

    @pl.when(j == 0)
    def _():
        x = x_ref[0]
        ms = jnp.mean(x * x, axis=-1, keepdims=True)
        y = x * lax.rsqrt(ms + EPS) * g_ref[...]
        h = y * (1.0 + mod_ref[0, 1:2, :]) + mod_ref[0, 0:1, :]
        h_scr[...] = h.astype(BF16)

    acc = _dot(h_scr[...], w_ref[...])
    o_ref[0] = acc.astype(BF16)

    @pl.when(j == COL_BA // tn)
    def _():
        ba_ref[0] = acc[:, COL_BA % tn:COL_BA % tn + LANES]


def _inproj_call(x, mod3, g_mix, w_in_p):
    b, s, d = x.shape
    n = w_in_p.shape[1]
    tm = min(s, 1024)
    tn = n // 4
    assert n % 4 == 0 and tn % LANES == 0 and COL_BA % tn + LANES <= tn
    return pl.pallas_call(
        functools.partial(_inproj_kernel, tn=tn),
        grid=(b, s // tm, n // tn),
        in_specs=[pl.BlockSpec((1, tm, d), lambda bi, i, j: (bi, i, 0)),
                  pl.BlockSpec((1, 6, d), lambda bi, i, j: (bi, 0, 0)),
                  pl.BlockSpec((1, d), lambda bi, i, j: (0, 0)),
                  pl.BlockSpec((d, tn), lambda bi, i, j: (0, j))],
        out_specs=[pl.BlockSpec((1, tm, tn), lambda bi, i, j: (bi, i, j)),
                   pl.BlockSpec((1, tm, LANES), lambda bi, i, j: (bi, i, 0))],
        out_shape=[jax.ShapeDtypeStruct((b, s, n), BF16),
                   jax.ShapeDtypeStruct((b, s, LANES), F32)],
        scratch_shapes=[pltpu.VMEM((tm, d), BF16)],
        compiler_params=_cparams(("parallel", "parallel", "arbitrary")),
    )(x, mod3, g_mix.reshape(1, d), w_in_p)


def _dn_kernel(q_ref, k_ref, v_ref, z_ref, ba_ref, cwq_ref, cwk_ref, cwv_ref, alog_ref, dt_ref, og_ref,
               y_ref,
               xp, qn, kn, vn, aq_s, b_s, egl_s, o_s, *, seq, tr, ppb):
    head = pl.program_id(1)
    n_pairs = seq // PAIR
    n_tiles = seq // tr

    def conv_phase(x_ref, cw_ref, dst, normalise, scale):
        xp[0:8, :] = jnp.zeros((8, LANES), F32)
        xp[seq + 8:seq + 16, :] = jnp.zeros((8, LANES), F32)

        def copy_tile(t, c):
            r0 = pl.multiple_of(t * tr, tr)
            xp[pl.ds(r0 + 8, tr), :] = x_ref[0, pl.ds(r0, tr), :].astype(F32)
            return c

        lax.fori_loop(0, n_tiles, copy_tile, 0)
        cw = cw_ref[...]

        def tile(t, c):
            r0 = pl.multiple_of(t * tr, tr)
            acc = xp[pl.ds(r0 + 6, tr), :] * cw[0:1]
            for kk in range(1, CONV_K):
                acc = acc + xp[pl.ds(r0 + 6 + kk, tr), :] * cw[kk:kk + 1]
            y = acc * _sigmoid(acc)
            if normalise:
                y = y * lax.rsqrt(jnp.sum(y * y, axis=-1, keepdims=True) + EPS)
            if scale != 1.0:
                y = y * scale
            dst[pl.ds(r0, tr), :] = y
            return c

        lax.fori_loop(0, n_tiles, tile, 0)

    conv_phase(q_ref, cwq_ref, qn, True, DN_HEAD_DIM ** -0.5)
    conv_phase(k_ref, cwk_ref, kn, True, 1.0)
    conv_phase(v_ref, cwv_ref, vn, False, 1.0)

    ri = lax.broadcasted_iota(I32, (PAIR, PAIR), 0)
    ci = lax.broadcasted_iota(I32, (PAIR, PAIR), 1)
    same = (ri // CHUNK) == (ci // CHUNK)
    mask_incl = (same & (ci <= ri), same & (ci >= ri))
    mask_strict = (same & (ci < ri), same & (ci > ri))
    eye = jnp.where(ri == ci, 1.0, 0.0).astype(F32)
    first_chunk = ci < CHUNK
    lu = jnp.concatenate([jnp.where(mask_incl[0], 1.0, 0.0), jnp.where(mask_incl[1], 1.0, 0.0)],
                         axis=0).astype(BF16)
    lu3 = jnp.concatenate([lu, lu, lu], axis=1)
    alog_row = alog_ref[...]
    dt_row = dt_ref[...]

    def lane_bcast(x, col):
        shifted = pltpu.roll(x, shift=lax.rem(LANES - col, LANES), axis=1)
        return jnp.broadcast_to(shifted[:, 0:1], x.shape)

    def wide(x):
        return jnp.concatenate([x, x], axis=1)

    same16 = wide((ri // 16) == (ci // 16))
    same32 = wide((ri // 32) == (ci // 32))
    off16 = same32 & jnp.logical_not(same16)
    eye_w = wide(eye)
    zero_blk = jnp.zeros((PAIR, PAIR), BF16)

    def bdiag(w):
        return jnp.concatenate([jnp.concatenate([w[:, :PAIR], zero_blk], axis=1),
                                jnp.concatenate([zero_blk, w[:, PAIR:]], axis=1)], axis=0)

    def tri_inverse_many(lws):
        d0s = [jnp.where(same16, lw, 0.0) for lw in lws]
        ms = [(-d0).astype(BF16) for d0 in d0s]
        xs = [eye_w - d0 for d0 in d0s]
        for _ in range(3):
            ms = [_dot(m, bdiag(m)).astype(BF16) for m in ms]
            xs = [x + _dot(x.astype(BF16), bdiag(m)) for x, m in zip(xs, ms)]
        for level in range(2):
            cs = [bdiag((jnp.where(off16, lw, 0.0) if level == 0 else jnp.where(same32, 0.0, lw)).astype(BF16))
                  for lw in lws]
            xbs = [x.astype(BF16) for x in xs]
            ts = [_dot(xb, c).astype(BF16) for xb, c in zip(xbs, cs)]
            xs = [x - _dot(t, bdiag(xb)) for x, t, xb in zip(xs, ts, xbs)]
        rs = []
        for lw, x in zip(lws, xs):
            lh, ll = _split2(lw)
            xh, xl = _split2(x)
            prod = _dot(jnp.concatenate([lh, lh, ll], axis=1),
                        jnp.concatenate([bdiag(xh), bdiag(xl), bdiag(xh)], axis=0))
            rs.append((eye_w - x) - prod)
        return [x + _dot(x.astype(BF16), bdiag(r.astype(BF16))) for x, r in zip(xs, rs)]

    def gate_stage(pairs):
        r0s = [pl.multiple_of(p * PAIR, PAIR) for p in pairs]
        blks = [ba_ref[0, pl.ds(r0, PAIR), :] for r0 in r0s]
        gs = [-jnp.exp(alog_row) * _softplus(blk + dt_row) for blk in blks]
        betas = [jnp.concatenate([lane_bcast(sg, head), lane_bcast(sg, DN_HEADS + head)], axis=1)
                 for sg in [_sigmoid(blk) for blk in blks]]
        g_reps = [jnp.concatenate([lane_bcast(g, 2 * DN_HEADS + head), lane_bcast(g, 3 * DN_HEADS + head)],
                                  axis=1) for g in gs]
        gps = [jnp.concatenate(_split3(g_rep), axis=0) for g_rep in g_reps]
        kqs = []
        for r0 in r0s:
            kb = kn[pl.ds(r0, PAIR), :].astype(BF16)
            kqs.append(_dot_nt(jnp.concatenate([kb, qn[pl.ds(r0, PAIR), :].astype(BF16)], axis=0), kb))
        css = [_dot(lu3, gp) for gp in gps]
        out = []
        for r0, beta, g_rep, cs, kq in zip(r0s, betas, g_reps, css, kqs):
            pre, suf = cs[:PAIR], cs[PAIR:]
            per_dir = ((beta[:, :LANES], pre[:, :LANES], suf[:, :LANES] - g_rep[:, :LANES]),
                       (beta[:, LANES:], suf[:, LANES:], pre[:, LANES:] - g_rep[:, LANES:]))
            out.append((r0, per_dir, kq[:PAIR], kq[PAIR:]))
        return out

    def finish_stage(chains, t_invs):
        egs, wus = [], []
        for (p, r0, d, beta, gc, ex, qk, dec), t_inv in zip(chains, t_invs):
            k2 = kn[pl.ds(r0, PAIR), :]
            eg = jnp.exp(gc)
            vb = (vn[pl.ds(r0, PAIR), :] * beta).astype(BF16)
            kbg = (k2 * beta * eg).astype(BF16)
            egs.append(eg)
            wus.append(_dot(t_inv.astype(BF16), jnp.concatenate([kbg, vb], axis=1)).astype(BF16))
        abs_, qos = [], []
        for (p, r0, d, beta, gc, ex, qk, dec), wu in zip(chains, wus):
            kst = (kn[pl.ds(r0, PAIR), :] * jnp.exp(ex)).T.astype(BF16)
            kst2 = jnp.concatenate([jnp.where(first_chunk, kst, 0.0), jnp.where(first_chunk, 0.0, kst)],
                                   axis=0).astype(BF16)
            abs_.append(_dot(kst2, wu))
            intra = jnp.where(mask_incl[d], qk * dec, 0.0).astype(BF16)
            qos.append(_dot(intra, wu))
        parts, comps = [], []
        for (p, r0, d, beta, gc, ex, qk, dec), eg, ab, qo in zip(chains, egs, abs_, qos):
            c1, c2 = (0, 1) if d == 0 else (1, 0)
            a = (-ab[:PAIR, :LANES], -ab[PAIR:, :LANES])
            b = (ab[:PAIR, LANES:], ab[PAIR:, LANES:])
            qp = qn[pl.ds(r0, PAIR), :] * eg - qo[:, :LANES]
            q = (qp[:CHUNK], qp[CHUNK:])
            comps.append(_dot(jnp.concatenate([a[c2], q[c2]], axis=0).astype(BF16),
                              jnp.concatenate([a[c1], b[c1]], axis=1).astype(BF16)))
            parts.append((a, b, q, c1, c2))
        for (p, r0, d, beta, gc, ex, qk, dec), qo, (a, b, q, c1, c2), comp in zip(chains, qos, parts, comps):
            etot = jnp.exp(gc + ex)
            e1, e2 = etot[c1 * CHUNK:c1 * CHUNK + 1], etot[c2 * CHUNK:c2 * CHUNK + 1]
            m = e2 * a[c1] + e1 * a[c2] + comp[:PAIR, :LANES]
            a0 = pl.multiple_of(p * AQ_ROWS, AQ_ROWS)
            aq_s[d, pl.ds(a0, PAIR), :] = m.astype(BF16)
            aq_s[d, pl.ds(a0 + PAIR, CHUNK), :] = q[c1].astype(BF16)
            aq_s[d, pl.ds(a0 + PAIR + CHUNK, CHUNK), :] = (e1 * q[c2] + comp[PAIR:, :LANES]).astype(BF16)
            b_s[d, pl.ds(r0, PAIR), :] = e2 * b[c1] + comp[:PAIR, LANES:] + b[c2]
            o1 = pl.ds(r0 + c1 * CHUNK, CHUNK)
            o2 = pl.ds(r0 + c2 * CHUNK, CHUNK)
            o_s[o1, :] = o_s[o1, :] + qo[c1 * CHUNK:(c1 + 1) * CHUNK, LANES:]
            o_s[o2, :] = o_s[o2, :] + qo[c2 * CHUNK:(c2 + 1) * CHUNK, LANES:] + comp[PAIR:, LANES:]
            e0 = pl.multiple_of(p * 8, 8)
            egl_s[d, pl.ds(e0, 8), :] = etot[0:8] * etot[CHUNK:CHUNK + 8]

    def prep_block(i, c):
        pairs = [i * ppb + j for j in range(ppb)]
        chains, lws = [], []
        for p, (r0, per_dir, kk, qk) in zip(pairs, gate_stage(pairs)):
            lms = []
            for d in range(2):
                beta, gc, ex = per_dir[d]
                m_in = mask_incl[d]
                dec = jnp.where(m_in, jnp.exp(jnp.where(m_in, gc - gc.T, 0.0)), 0.0)
                lms.append(jnp.where(mask_strict[d], beta * kk * dec, 0.0))
                chains.append((p, r0, d, beta, gc, ex, qk, dec))
            lws.append(jnp.concatenate(lms, axis=1))
        t_invs = []
        for t_w in tri_inverse_many(lws):
            t_invs += [t_w[:, :PAIR], t_w[:, PAIR:]]
        finish_stage(chains, t_invs)
        return c

    def zero_tile(t, c):
        r0 = pl.multiple_of(t * tr, tr)
        o_s[pl.ds(r0, tr), :] = jnp.zeros((tr, LANES), F32)
        return c

    lax.fori_loop(0, n_tiles, zero_tile, 0)
    lax.fori_loop(0, n_pairs // ppb, prep_block, 0)

    def pair_step(d, pair, state):
        c1 = d
        r0 = pl.multiple_of(pair * PAIR, PAIR)
        a0 = pl.multiple_of(pair * AQ_ROWS, AQ_ROWS)
        res = _dot(aq_s[d, pl.ds(a0, AQ_ROWS), :], state.astype(BF16))
        o1 = pl.ds(r0 + c1 * CHUNK, CHUNK)
        o2 = pl.ds(r0 + (1 - c1) * CHUNK, CHUNK)
        o_s[o1, :] = o_s[o1, :] + res[PAIR:PAIR + CHUNK]
        o_s[o2, :] = o_s[o2, :] + res[PAIR + CHUNK:]
        eg = egl_s[d, pl.ds(pl.multiple_of(pair * 8, 8), 8), :][0:1, :]
        return state * eg + res[:PAIR] + b_s[d, pl.ds(r0, PAIR), :]

    def pair_scan(n, carry):
        s_f, s_b = carry
        return pair_step(0, n, s_f), pair_step(1, n_pairs - 1 - n, s_b)

    zero_state = jnp.zeros((DN_HEAD_DIM, DN_HEAD_DIM), F32)
    lax.fori_loop(0, n_pairs, pair_scan, (zero_state, zero_state))

    og = og_ref[...]

    def out_tile(t, c):
        r0 = pl.multiple_of(t * tr, tr)
        o = o_s[pl.ds(r0, tr), :]
        z = z_ref[0, pl.ds(r0, tr), :].astype(F32)
        y = o * lax.rsqrt(jnp.mean(o * o, axis=-1, keepdims=True) + EPS) * og
        y_ref[0, pl.ds(r0, tr), :] = (y * (z * _sigmoid(z))).astype(BF16)
        return c

    lax.fori_loop(0, n_tiles, out_tile, 0)


def _dn_call(proj, ba, conv_w, alog_row, dt_row, o_gain):
    b, s, _ = proj.shape
    hd = DN_HEAD_DIM
    tr = min(s, 1024)
    qb, kb, vb, zb = COL_Q // hd, COL_K // hd, COL_V // hd, COL_Z // hd

    def col(base):
        return pl.BlockSpec((1, s, hd), lambda bi, h: (bi, 0, base + h))

    def cw(base):
        return pl.BlockSpec((CONV_K, hd), lambda bi, h: (0, base + h))

    row = pl.BlockSpec((1, LANES), lambda bi, h: (0, 0))
    n_pairs = s // PAIR
    ppb = next(n for n in (8, 4, 2, 1) if n_pairs % n == 0)
    kern = functools.partial(_dn_kernel, seq=s, tr=tr, ppb=ppb)
    return pl.pallas_call(
        kern,
        grid=(b, DN_HEADS),
        in_specs=[col(qb), col(kb), col(vb), col(zb),
                  pl.BlockSpec((1, s, LANES), lambda bi, h: (bi, 0, 0)),
                  cw(qb), cw(kb), cw(vb), row, row, row],
        out_specs=pl.BlockSpec((1, s, hd), lambda bi, h: (bi, 0, h)),
        out_shape=jax.ShapeDtypeStruct((b, s, DN_HEADS * hd), BF16),
        scratch_shapes=[pltpu.VMEM((s + 16, LANES), F32),
                        pltpu.VMEM((s, hd), F32), pltpu.VMEM((s, hd), F32), pltpu.VMEM((s, hd), F32),
                        pltpu.VMEM((2, s // PAIR * AQ_ROWS, hd), BF16),
                        pltpu.VMEM((2, s // PAIR * hd, hd), F32),
                        pltpu.VMEM((2, s // PAIR * 8, LANES), F32),
                        pltpu.VMEM((s, hd), F32)],
        compiler_params=_cparams(("parallel", "parallel")),
    )(proj, proj, proj, proj, ba, conv_w, conv_w, conv_w, alog_row, dt_row, o_gain.reshape(1, hd))


def _mlaproj_kernel(cq_ref, ckv_ref, kr_ref, pos_ref, invf_ref, sgn_ref, qg_ref, kvg_ref,
                    wqn_ref, wqr_ref, wqs_ref, wkn_ref, wv_ref,
                    q_ref, k_ref, v_ref):
    cq = cq_ref[0].astype(F32)
    hq = (cq * lax.rsqrt(jnp.mean(cq * cq, axis=-1, keepdims=True) + EPS) * qg_ref[...]).astype(BF16)
    qn = _dot(hq, wqn_ref[...])
    qr = _dot(hq, wqr_ref[...])
    qs = _dot(hq, wqs_ref[...])
    ang = pos_ref[0].astype(F32) * invf_ref[...]
    cc = jnp.cos(ang)
    ss = jnp.sin(ang) * sgn_ref[...]
    scale = QK_DIM ** -0.5 * math.log2(math.e)
    ckv = ckv_ref[0].astype(F32)
    hkv = (ckv * lax.rsqrt(jnp.mean(ckv * ckv, axis=-1, keepdims=True) + EPS) * kvg_ref[...]).astype(BF16)
    kn = _dot(hkv, wkn_ref[...])
    vv = _dot(hkv, wv_ref[...])
    kr = kr_ref[0].astype(F32)
    k_rope = (kr[:, :ROPE_DIM] * cc + kr[:, ROPE_DIM:] * ss).astype(BF16)
    for h in range(MLA_HEADS):
        lo, hi = h * ROPE_DIM, (h + 1) * ROPE_DIM
        q_rope = qr[:, lo:hi] * cc + qs[:, lo:hi] * ss
        q_ref[0, h, :, 0:NOPE_DIM] = (qn[:, h * NOPE_DIM:(h + 1) * NOPE_DIM] * scale).astype(BF16)
        q_ref[0, h, :, NOPE_DIM:QK_DIM] = (q_rope * scale).astype(BF16)
        k_ref[0, h, :, 0:NOPE_DIM] = kn[:, h * NOPE_DIM:(h + 1) * NOPE_DIM].astype(BF16)
        k_ref[0, h, :, NOPE_DIM:QK_DIM] = k_rope
        v_ref[0, h] = vv[:, h * V_DIM:(h + 1) * V_DIM].astype(BF16)


def _mlaproj_call(proj, pos3, invf, sgn, q_gain, kv_gain, wqn, wqr, wqs, wkn, wv):
    b, s, _ = proj.shape
    tm = min(s, 512)
    nh = MLA_HEADS

    def full(a):
        return pl.BlockSpec(a.shape, lambda bi, i: (0,) * a.ndim)

    qg = q_gain.reshape(1, Q_LORA)
    kvg = kv_gain.reshape(1, KV_LORA)
    return pl.pallas_call(
        _mlaproj_kernel,
        grid=(b, s // tm),
        in_specs=[pl.BlockSpec((1, tm, Q_LORA), lambda bi, i: (bi, i, COL_CQ // Q_LORA)),
                  pl.BlockSpec((1, tm, KV_LORA), lambda bi, i: (bi, i, COL_CKV // KV_LORA)),
                  pl.BlockSpec((1, tm, LANES), lambda bi, i: (bi, i, COL_KR // LANES)),
                  pl.BlockSpec((1, tm, 1), lambda bi, i: (bi, i, 0)),
                  full(invf), full(sgn), full(qg), full(kvg),
                  full(wqn), full(wqr), full(wqs), full(wkn), full(wv)],
        out_specs=[pl.BlockSpec((1, nh, tm, QK_DIM), lambda bi, i: (bi, 0, i, 0)),
                   pl.BlockSpec((1, nh, tm, QK_DIM), lambda bi, i: (bi, 0, i, 0)),
                   pl.BlockSpec((1, nh, tm, V_DIM), lambda bi, i: (bi, 0, i, 0))],
        out_shape=[jax.ShapeDtypeStruct((b, nh, s, QK_DIM), BF16),
                   jax.ShapeDtypeStruct((b, nh, s, QK_DIM), BF16),
                   jax.ShapeDtypeStruct((b, nh, s, V_DIM), BF16)],
        compiler_params=_cparams(("parallel", "parallel")),
    )(proj, proj, proj, pos3, invf, sgn, qg, kvg, wqn, wqr, wqs, wkn, wv)


def _attn_kernel(q_ref, k_ref, v_ref, o_ref, *, seq, tq, sub, tk):
    n_sub, n_c = tq // sub, seq // tk

    def scores(a, c):
        return _dot_nt(q_ref[0, 0, a * sub:(a + 1) * sub, :], k_ref[0, 0, c * tk:(c + 1) * tk, :])

    s_cur = [scores(0, c) for c in range(n_c)]
    for a in range(n_sub):
        mx = s_cur[0]
        for c in range(1, n_c):
            mx = jnp.maximum(mx, s_cur[c])
        m = jnp.max(mx, axis=-1, keepdims=True)
        lp = jnp.zeros((sub, LANES), F32)
        acc = jnp.zeros((sub, V_DIM), F32)
        s_next = []
        for c in range(n_c):
            p = jnp.exp2(s_cur[c] - m)
            for j in range(tk // LANES):
                lp = lp + p[:, j * LANES:(j + 1) * LANES]
            acc = acc + _dot(p.astype(BF16), v_ref[0, 0, c * tk:(c + 1) * tk, :])
            if a + 1 < n_sub:
                s_next.append(scores(a + 1, c))
        l = jnp.sum(lp, axis=-1, keepdims=True)
        o_ref[0, a * sub:(a + 1) * sub, :] = (acc / l).astype(BF16)
        s_cur = s_next


def _attn_call(q, k, v):
    b, nh, s, _ = q.shape
    tq = min(s, 2048)
    kern = functools.partial(_attn_kernel, seq=s, tq=tq, sub=min(tq, 256), tk=min(s, 512))
    return pl.pallas_call(
        kern,
        grid=(b, nh, s // tq),
        in_specs=[pl.BlockSpec((1, 1, tq, QK_DIM), lambda bi, h, i: (bi, h, i, 0)),
                  pl.BlockSpec((1, 1, s, QK_DIM), lambda bi, h, i: (bi, h, 0, 0)),
                  pl.BlockSpec((1, 1, s, V_DIM), lambda bi, h, i: (bi, h, 0, 0))],
        out_specs=pl.BlockSpec((1, tq, V_DIM), lambda bi, h, i: (bi, i, h)),
        out_shape=jax.ShapeDtypeStruct((b, s, nh * V_DIM), BF16),
        compiler_params=_cparams(("parallel", "parallel", "arbitrary")),
    )(q, k, v)


def _merge_kernel(ydn_ref, ymla_ref, gdn_ref, gmla_ref, x_ref, mod_ref, wodn_ref, womla_ref, wout_ref,
                  gffn_ref, wrt_ref,
                  x1_ref, h2_ref, aff_ref):
    y_dn = _dot(ydn_ref[0], wodn_ref[...])
    y_mla = _dot(ymla_ref[0], womla_ref[...])
    merged = _sigmoid(gdn_ref[0].astype(F32)) * y_dn + _sigmoid(gmla_ref[0].astype(F32)) * y_mla
    x1 = x_ref[0] + mod_ref[0, 2:3, :] * _dot(merged.astype(BF16), wout_ref[...])
    x1_ref[0] = x1
    y = x1 * lax.rsqrt(jnp.mean(x1 * x1, axis=-1, keepdims=True) + EPS) * gffn_ref[...]
    h2 = (y * (1.0 + mod_ref[0, 4:5, :]) + mod_ref[0, 3:4, :]).astype(BF16)
    h2_ref[0] = h2
    logits = _dot_nt(wrt_ref[...], h2)
    mx = jnp.max(logits, axis=0, keepdims=True)
    ex = jnp.exp(logits - mx)
    aff_ref[0] = ex / jnp.sum(ex, axis=0, keepdims=True)


def _merge_call(y_dn, y_mla, proj, x, mod3, w_o_dn, w_o_mla, w_out, g_ffn, w_router_t):
    b, s, d = x.shape
    tm = min(s, 1024)
    ne = w_router_t.shape[0]

    def tok(width):
        return pl.BlockSpec((1, tm, width), lambda bi, i: (bi, i, 0))

    def full(a):
        return pl.BlockSpec(a.shape, lambda bi, i: (0,) * a.ndim)

    gf = g_ffn.reshape(1, d)
    return pl.pallas_call(
        _merge_kernel,
        grid=(b, s // tm),
        in_specs=[tok(d), tok(d),
                  pl.BlockSpec((1, tm, d), lambda bi, i: (bi, i, COL_GATE // d)),
                  pl.BlockSpec((1, tm, d), lambda bi, i: (bi, i, COL_GATE // d + 1)),
                  tok(d),
                  pl.BlockSpec((1, 6, d), lambda bi, i: (bi, 0, 0)),
                  full(w_o_dn), full(w_o_mla), full(w_out), full(gf), full(w_router_t)],
        out_specs=[tok(d), tok(d), pl.BlockSpec((1, ne, tm), lambda bi, i: (bi, 0, i))],
        out_shape=[jax.ShapeDtypeStruct((b, s, d), F32),
                   jax.ShapeDtypeStruct((b, s, d), BF16),
                   jax.ShapeDtypeStruct((b, ne, s), F32)],
        compiler_params=_cparams(("parallel", "parallel")),
    )(y_dn, y_mla, proj, proj, x, mod3, w_o_dn, w_o_mla, w_out, gf, w_router_t)


def _route_kernel(aff_ref, pos_ref, *, seq, cap):
    a = aff_ref[0]
    ne = a.shape[0]
    capf = jnp.float32(cap)

    def count_ge(t_bits):
        t = pltpu.bitcast(t_bits, F32)
        return jnp.sum(jnp.where(a >= t, 1.0, 0.0), axis=1, keepdims=True)

    def search(_, carry):
        lo, hi = carry
        mid = lo + ((hi - lo + 1) >> 1)
        ok = count_ge(mid) >= capf
        return jnp.where(ok, mid, lo), jnp.where(ok, hi, mid - 1)

    lo0 = jnp.zeros((ne, 1), I32)
    hi0 = jnp.full((ne, 1), 0x7F800000, I32)
    thr_bits, _ = lax.fori_loop(0, 32, search, (lo0, hi0))
    thr = pltpu.bitcast(thr_bits, F32)

    ur = lax.broadcasted_iota(I32, (LANES, LANES), 0)
    uc = lax.broadcasted_iota(I32, (LANES, LANES), 1)
    upper = jnp.where(ur <= uc, 1.0, 0.0).astype(BF16)

    def prefix_incl(mask):
        carry = jnp.zeros((ne, 1), F32)
        outs = []
        for j in range(seq // LANES):
            pr = _dot(mask[:, j * LANES:(j + 1) * LANES].astype(BF16), upper) + carry
            outs.append(pr)
            carry = pr[:, LANES - 1:LANES]
        return jnp.concatenate(outs, axis=1)

    gt = a > thr
    eq = a == thr
    n_gt = jnp.sum(jnp.where(gt, 1.0, 0.0), axis=1, keepdims=True)
    take_eq = eq & (prefix_incl(jnp.where(eq, 1.0, 0.0)) <= capf - n_gt)
    sel = gt | take_eq
    slot = prefix_incl(jnp.where(sel, 1.0, 0.0)) - 1.0
    pos_ref[0] = jnp.where(sel, slot, -1.0).astype(I32)


def _route_call(aff, cap):
    b, ne, s = aff.shape
    kern = functools.partial(_route_kernel, seq=s, cap=cap)
    return pl.pallas_call(
        kern,
        grid=(b,),
        in_specs=[pl.BlockSpec((1, ne, s), lambda bi: (bi, 0, 0))],
        out_specs=pl.BlockSpec((1, ne, s), lambda bi: (bi, 0, 0)),
        out_shape=jax.ShapeDtypeStruct((b, ne, s), I32),
        compiler_params=_cparams(("parallel",)),
    )(aff)


def _gather_kernel(pos_ref, aff_ref, h_ref, xe_ref, gate_ref, *, seq, cap, tk):
    slot = lax.broadcasted_iota(I32, (cap, 1), 0)
    gpart = jnp.zeros((cap, LANES), F32)
    onehots = []
    for kt in range(seq // tk):
        pm = pos_ref[0, 0, :, kt * tk:(kt + 1) * tk]
        hit = pm == slot
        onehots.append(jnp.where(hit, 1.0, 0.0).astype(BF16))
        picked = jnp.where(hit, aff_ref[0, 0, :, kt * tk:(kt + 1) * tk], 0.0)
        for j in range(tk // LANES):
            gpart = gpart + picked[:, j * LANES:(j + 1) * LANES]
    xe_ref[0, 0] = _dot(jnp.concatenate(onehots, axis=1), h_ref[0]).astype(BF16)
    gate_ref[0, 0] = jnp.broadcast_to(jnp.sum(gpart, axis=1, keepdims=True), (cap, LANES))


def _gather_call(pos4, aff4, h2, cap):
    b, ne, _, s = pos4.shape
    d = h2.shape[-1]
    tk = min(s, 512)
    kern = functools.partial(_gather_kernel, seq=s, cap=cap, tk=tk)
    return pl.pallas_call(
        kern,
        grid=(b, ne),
        in_specs=[pl.BlockSpec((1, 1, 1, s), lambda bi, e: (bi, e, 0, 0)),
                  pl.BlockSpec((1, 1, 1, s), lambda bi, e: (bi, e, 0, 0)),
                  pl.BlockSpec((1, s, d), lambda bi, e: (bi, 0, 0))],
        out_specs=[pl.BlockSpec((1, 1, cap, d), lambda bi, e: (bi, e, 0, 0)),
                   pl.BlockSpec((1, 1, cap, LANES), lambda bi, e: (bi, e, 0, 0))],
        out_shape=[jax.ShapeDtypeStruct((b, ne, cap, d), BF16),
                   jax.ShapeDtypeStruct((b, ne, cap, LANES), F32)],
        compiler_params=_cparams(("parallel", "arbitrary")),
    )(pos4, aff4, h2)


def _ffn_kernel(xe_ref, gate_ref, wg_ref, wu_ref, wd_ref, ye_ref):
    x = xe_ref[0, 0]
    g = _dot(x, wg_ref[0].astype(BF16))
    u = _dot(x, wu_ref[0].astype(BF16))
    hid = (g * _sigmoid(g)) * u
    y = _dot(hid.astype(BF16), wd_ref[0].astype(BF16))
    ye_ref[0, 0] = (y * gate_ref[0, 0][:, 0:1]).astype(BF16)


def _ffn_call(xe, gate, wg, wu, wd):
    b, ne, cap, d = xe.shape
    f = wg.shape[-1]
    return pl.pallas_call(
        _ffn_kernel,
        grid=(ne, b),
        in_specs=[pl.BlockSpec((1, 1, cap, d), lambda e, bi: (bi, e, 0, 0)),
                  pl.BlockSpec((1, 1, cap, LANES), lambda e, bi: (bi, e, 0, 0)),
                  pl.BlockSpec((1, d, f), lambda e, bi: (e, 0, 0)),
                  pl.BlockSpec((1, d, f), lambda e, bi: (e, 0, 0)),
                  pl.BlockSpec((1, f, d), lambda e, bi: (e, 0, 0))],
        out_specs=pl.BlockSpec((1, 1, cap, d), lambda e, bi: (bi, e, 0, 0)),
        out_shape=jax.ShapeDtypeStruct((b, ne, cap, d), BF16),
        compiler_params=_cparams(("parallel", "arbitrary")),
    )(xe, gate, wg, wu, wd)


def _combine_kernel(pos_ref, ye_ref, x1_ref, mod_ref, gfin_ref, o_ref, acc_ref, *, cap, last_norm):
    g = pl.program_id(2)
    eg = ye_ref.shape[1]

    @pl.when(g == 0)
    def _():
        acc_ref[...] = jnp.zeros_like(acc_ref)

    slot = lax.broadcasted_iota(I32, (1, cap), 1)
    pos = pos_ref[0, 0]
    onehot = jnp.concatenate(
        [jnp.where(pos[:, j:j + 1] == slot, 1.0, 0.0).astype(BF16) for j in range(eg)], axis=1)
    acc_ref[...] += _dot(onehot, ye_ref[0].reshape(eg * cap, ye_ref.shape[-1]))

    @pl.when(g == pl.num_programs(2) - 1)
    def _():
        x2 = x1_ref[0] + mod_ref[0, 5:6, :] * acc_ref[...]
        if last_norm:
            x2 = x2 * lax.rsqrt(jnp.mean(x2 * x2, axis=-1, keepdims=True) + EPS) * gfin_ref[...]
        o_ref[0] = x2


def _combine_call(pos_tok, ye, x1, mod3, g_final, cap, last_norm):
    b, s, d = x1.shape
    ne = ye.shape[1]
    n_groups, eg = pos_tok.shape[1], pos_tok.shape[3]
    tm = min(s, 512)
    kern = functools.partial(_combine_kernel, cap=cap, last_norm=last_norm)
    return pl.pallas_call(
        kern,
        grid=(b, s // tm, n_groups),
        in_specs=[pl.BlockSpec((1, 1, tm, eg), lambda bi, i, e: (bi, e, i, 0)),
                  pl.BlockSpec((1, eg, cap, d), lambda bi, i, e: (bi, e, 0, 0)),
                  pl.BlockSpec((1, tm, d), lambda bi, i, e: (bi, i, 0)),
                  pl.BlockSpec((1, 6, d), lambda bi, i, e: (bi, 0, 0)),
                  pl.BlockSpec((1, d), lambda bi, i, e: (0, 0))],
        out_specs=pl.BlockSpec((1, tm, d), lambda bi, i, e: (bi, i, 0)),
        out_shape=jax.ShapeDtypeStruct((b, s, d), F32),
        scratch_shapes=[pltpu.VMEM((tm, d), F32)],
        compiler_params=_cparams(("parallel", "parallel", "arbitrary")),
    )(pos_tok, ye, x1, mod3, g_final.reshape(1, d))


def _pack_w_in(w):
    o_ba = 4 * 1024 + 0
    o_b = o_ba
    o_a = o_b + 2 * DN_HEADS
    o_cq = o_a + 2 * DN_HEADS
    o_ckv = o_cq + Q_LORA
    o_kr = o_ckv + KV_LORA
    o_g = o_kr + ROPE_DIM
    half = ROPE_DIM // 2
    w_kr = w[:, o_kr:o_g]
    w_kr_sw = jnp.concatenate([w_kr[:, half:], w_kr[:, :half]], axis=1)
    pad = jnp.zeros((w.shape[0], COL_GATE - COL_BA - 4 * DN_HEADS), w.dtype)
    packed = jnp.concatenate([w[:, :o_ba], w[:, o_cq:o_ckv], w[:, o_ckv:o_kr], w_kr, w_kr_sw,
                              w[:, o_b:o_a], w[:, o_a:o_cq], pad, w[:, o_g:]], axis=1)
    assert packed.shape[1] == N_PAD
    return packed.astype(BF16)


def _lane_row(vals, offset):
    row = jnp.zeros((1, LANES), F32)
    return row.at[0, offset:offset + vals.size].set(vals.reshape(-1).astype(F32))


def kernel(x, c, positions, w_mod, b_mod, g_mix, w_in, conv_w, a_log, dt_bias, dn_o_gain, q_gain, w_uq,
           kv_gain, w_ukv, w_o_dn, w_o_mla, w_out, g_ffn, w_router, w_gate, w_up, w_down, g_final):
    b, s, d = x.shape
    depth = w_mod.shape[0]
    cap = CAPACITY_FACTOR * s // N_EXPERTS
    half = ROPE_DIM // 2
    inv_freq = ROPE_THETA ** (-jnp.arange(half, dtype=F32) / half)
    invf = jnp.concatenate([inv_freq, inv_freq]).reshape(1, ROPE_DIM)
    sgn = jnp.concatenate([-jnp.ones((half,), F32), jnp.ones((half,), F32)]).reshape(1, ROPE_DIM)
    pos3 = positions.reshape(b, s, 1)
    c_pad = jnp.zeros((8, d), F32).at[:b].set(c)
    swap = np.concatenate([np.arange(half, ROPE_DIM), np.arange(half)])

    for l in range(depth):
        mod3 = _mod_call(c_pad, w_mod[l], b_mod[l])[:b].reshape(b, 6, d)
        proj, ba = _inproj_call(x, mod3, g_mix[l], _pack_w_in(w_in[l]))

        alog_row = _lane_row(a_log[l], 2 * DN_HEADS)
        dt_row = _lane_row(dt_bias[l], 2 * DN_HEADS)
        y_dn = _dn_call(proj, ba, conv_w[l], alog_row, dt_row, dn_o_gain[l])

        wq = w_uq[l].reshape(Q_LORA, MLA_HEADS, QK_DIM)
        wqn = wq[:, :, :NOPE_DIM].reshape(Q_LORA, MLA_HEADS * NOPE_DIM).astype(BF16)
        wqr = wq[:, :, NOPE_DIM:].reshape(Q_LORA, MLA_HEADS * ROPE_DIM).astype(BF16)
        wqs = wq[:, :, NOPE_DIM:][:, :, swap].reshape(Q_LORA, MLA_HEADS * ROPE_DIM).astype(BF16)
        wkv = w_ukv[l].reshape(KV_LORA, MLA_HEADS, NOPE_DIM + V_DIM)
        wkn = wkv[:, :, :NOPE_DIM].reshape(KV_LORA, MLA_HEADS * NOPE_DIM).astype(BF16)
        wv = wkv[:, :, NOPE_DIM:].reshape(KV_LORA, MLA_HEADS * V_DIM).astype(BF16)
        q, k, v = _mlaproj_call(proj, pos3, invf, sgn, q_gain[l], kv_gain[l], wqn, wqr, wqs, wkn, wv)
        y_mla = _attn_call(q, k, v)

        x1, h2, aff = _merge_call(y_dn, y_mla, proj, x, mod3, w_o_dn[l].astype(BF16),
                                  w_o_mla[l].astype(BF16), w_out[l].astype(BF16), g_ffn[l],
                                  w_router[l].T.astype(BF16))
        pos = _route_call(aff, cap)
        pos4 = pos.reshape(b, N_EXPERTS, 1, s)
        aff4 = aff.reshape(b, N_EXPERTS, 1, s)
        xe, gate = _gather_call(pos4, aff4, h2, cap)
        ye = _ffn_call(xe, gate, w_gate[l], w_up[l], w_down[l])
        pos_tok = pos.reshape(b, N_EXPERTS // COMBINE_GROUP, COMBINE_GROUP, s).transpose(0, 1, 3, 2)
        x = _combine_call(pos_tok, ye, x1, mod3, g_final, cap, last_norm=(l == depth - 1))
    return x
```
